```python
import math
import jax, jax.numpy as jnp
from jax import lax
import numpy as np

D_MODEL = 2048
BATCH = 4
SEQ = 4096
DEPTH = 2
DEC_BATCH = 32
DEC_SEQ = 16
PAST_LEN = 2048

CHUNK = 64
Q_BLOCK = 128
EPS = 1e-6
N_BRANCH = 4
W_A = D_MODEL // 4
CONV_A = 3
H_B = 4
DQ_B = D_MODEL // 4 // (2 * H_B)
DV_B = 2 * DQ_B
H_C = 4
DH_C = D_MODEL // 4 // H_C
FORGET_BIAS = 3.0
D_INNER = D_MODEL // 4
P_D = 64
H_D = D_INNER // P_D
G_D = 2
N_D = 128
CONV_D_W = 4
CONV_D = D_INNER + 2 * G_D * N_D
SSD_CHUNK = 64
D_FF = 4 * D_MODEL
SECTION_SIZES = (W_A, W_A, W_A,
                 2 * H_B * DQ_B, 2 * H_B * DQ_B, H_B * DV_B,
                 H_C * DH_C, H_C * DH_C, H_C * DH_C, H_C,
                 D_INNER, CONV_D, H_D,
                 N_BRANCH * D_MODEL)
N_IN = sum(SECTION_SIZES)

kernel_name = 'hybrid_streaming_encoder_step'


def rmsnorm(x, g):
    xf = x.astype(jnp.float32)
    y = xf * lax.rsqrt(jnp.mean(xf * xf, axis=-1, keepdims=True) + EPS)
    return (y * g.astype(jnp.float32)).astype(x.dtype)


def split_cols(u):
    parts, start = [], 0
    for size in SECTION_SIZES:
        parts.append(u[..., start:start + size])
        start += size
    return parts


def causal_dwconv(x, prev, w):
    K = w.shape[0]
    T = x.shape[1]
    xp = jnp.concatenate([prev.astype(x.dtype), x], axis=1)
    y = xp[:, 0:T] * w[0]
    for k in range(1, K):
        y = y + xp[:, k:k + T] * w[k]
    return y, xp[:, xp.shape[1] - (K - 1):]


def sweep_query_blocks(fn, qs, q_pos):
    T = q_pos.shape[0]
    if T <= Q_BLOCK or T % Q_BLOCK:
        return fn(qs, q_pos)
    nb = T // Q_BLOCK
    qs_b = tuple(jnp.moveaxis(q.reshape(q.shape[0], nb, Q_BLOCK, *q.shape[2:]), 1, 0) for q in qs)
    pos_b = q_pos.reshape(nb, Q_BLOCK)
    out = lax.map(lambda a: fn(a[0], a[1]), (qs_b, pos_b))
    out = jnp.moveaxis(out, 0, 1)
    return out.reshape(out.shape[0], T, *out.shape[3:])


def short_conv_mixer(gate_b, gate_c, h, prev, w):
    u = gate_c * h
    v, new_prev = causal_dwconv(u, prev, w)
    return gate_b * v, new_prev


def diff_attention(q, k, v, q_pos, k_pos, lam_params, subln_g, li):
    lam_init = 0.8 - 0.6 * math.exp(-0.3 * li)
    lp = lam_params.astype(jnp.float32)
    lam = jnp.exp(jnp.sum(lp[0] * lp[1])) - jnp.exp(jnp.sum(lp[2] * lp[3])) + lam_init
    b, T = q.shape[0], q.shape[1]
    q = q.reshape(b, T, H_B, 2, DQ_B)
    k = k.reshape(b, k.shape[1], H_B, 2, DQ_B)
    k1, k2 = k[..., 0, :], k[..., 1, :]
    k_chunk = k_pos // CHUNK
    scale = DQ_B ** -0.5

    def block(qs, qp):
        q1, q2 = qs
        mask = k_chunk[None, :] <= (qp // CHUNK)[:, None]

        def probs(qq, kk):
            s = jnp.einsum('bqhd,bkhd->bhqk', qq, kk).astype(jnp.float32) * scale
            return jax.nn.softmax(jnp.where(mask, s, -jnp.inf), axis=-1)

        p = probs(q1, k1) - lam * probs(q2, k2)
        return jnp.einsum('bhqk,bkhd->bqhd', p.astype(v.dtype), v)

    o = sweep_query_blocks(block, (q[..., 0, :], q[..., 1, :]), q_pos)
    o = rmsnorm(o, subln_g) * (1.0 - lam_init)
    return o.reshape(b, T, H_B * DV_B)


def forgetting_attention(q, k, v, logf, q_pos, k_pos):
    b, T = q.shape[0], q.shape[1]
    cum = jnp.cumsum(logf, axis=1)
    q_cum = cum[:, cum.shape[1] - T:]
    k_cum = jnp.transpose(cum, (0, 2, 1))
    scale = DH_C ** -0.5

    def block(qs, qp):
        qq, qc = qs
        s = jnp.einsum('bqhd,bkhd->bhqk', qq, k).astype(jnp.float32) * scale
        s = s + jnp.transpose(qc, (0, 2, 1))[..., None] - k_cum[:, :, None, :]
        mask = k_pos[None, :] <= qp[:, None]
        p = jax.nn.softmax(jnp.where(mask, s, -jnp.inf), axis=-1)
        return jnp.einsum('bhqk,bkhd->bqhd', p.astype(v.dtype), v)

    o = sweep_query_blocks(block, (q, q_cum), q_pos)
    return o.reshape(b, T, H_C * DH_C)


def ssd_scan(x, dt, a, bm, cm, s0):
    b, T, h, p = x.shape
    L = SSD_CHUNK
    pad = (-T) % L
    f32 = jnp.float32

    def padt(t):
        return jnp.pad(t, [(0, 0), (0, pad)] + [(0, 0)] * (t.ndim - 2))

    rep = h // G_D
    xf = padt(x.astype(f32))
    dtf = padt(dt)
    bf = padt(jnp.repeat(bm.astype(f32), rep, axis=2))
    cf = padt(jnp.repeat(cm.astype(f32), rep, axis=2))
    nc = (T + pad) // L
    xc = xf.reshape(b, nc, L, h, p)
    dtc = dtf.reshape(b, nc, L, h)
    bc = bf.reshape(b, nc, L, h, N_D)
    cc = cf.reshape(b, nc, L, h, N_D)
    a_cum = jnp.cumsum(dtc * a, axis=2)
    causal = jnp.tril(jnp.ones((L, L), dtype=bool))
    seg = a_cum[:, :, :, None, :] - a_cum[:, :, None, :, :]
    decay_in = jnp.exp(jnp.where(causal[None, None, :, :, None], seg, -jnp.inf))
    scores = jnp.einsum('bclhn,bcshn->bclsh', cc, bc) * decay_in
    y_diag = jnp.einsum('bclsh,bcsh,bcshp->bclhp', scores, dtc, xc)
    decay_end = jnp.exp(a_cum[:, :, L - 1:, :] - a_cum)
    chunk_states = jnp.einsum('bclhn,bclh,bclhp->bchpn', bc, decay_end * dtc, xc)
    chunk_decay = jnp.exp(a_cum[:, :, L - 1, :])

    def step(s, inp):
        st, dec = inp
        return s * dec[:, :, None, None] + st, s

    s_final, s_enter = lax.scan(step, s0.astype(f32),
                                (jnp.moveaxis(chunk_states, 1, 0), jnp.moveaxis(chunk_decay, 1, 0)))
    s_enter = jnp.moveaxis(s_enter, 0, 1)
    y_off = jnp.einsum('bclhn,bchpn,bclh->bclhp', cc, s_enter, jnp.exp(a_cum))
    y = (y_diag + y_off).reshape(b, nc * L, h, p)[:, :T]
    return y, s_final


def mamba2_mixer(z, xbc, dt_raw, conv_prev, ssm_prev, conv_w, conv_b, dt_bias, a_log, d_skip, norm_g):
    b, T = z.shape[0], z.shape[1]
    xbc_c, conv_new = causal_dwconv(xbc, conv_prev, conv_w)
    xbc_c = jax.nn.silu(xbc_c + conv_b)
    xs = xbc_c[..., :D_INNER].reshape(b, T, H_D, P_D)
    bm = xbc_c[..., D_INNER:D_INNER + G_D * N_D].reshape(b, T, G_D, N_D)
    cm = xbc_c[..., D_INNER + G_D * N_D:].reshape(b, T, G_D, N_D)
    dt = jax.nn.softplus(dt_raw.astype(jnp.float32) + dt_bias.astype(jnp.float32))
    a = -jnp.exp(a_log.astype(jnp.float32))
    y, s_new = ssd_scan(xs, dt, a, bm, cm, ssm_prev)
    y = y + d_skip.astype(jnp.float32)[:, None] * xs.astype(jnp.float32)
    gated = y.reshape(b, T, D_INNER) * jax.nn.silu(z.astype(jnp.float32))
    gg = gated.reshape(b, T, G_D, D_INNER // G_D)
    gg = gg * lax.rsqrt(jnp.mean(gg * gg, axis=-1, keepdims=True) + EPS)
    out = gg.reshape(b, T, D_INNER) * norm_g.astype(jnp.float32)
    return out.astype(z.dtype), conv_new, s_new.astype(z.dtype)


def trunk_layer(x, li, lp, st):
    conv_a_prev, dk_c, dv_c, fk_c, fv_c, flf_c, sconv_prev, ssm_prev = st
    b, T, _ = x.shape
    past = dk_c.shape[1]
    q_pos = past + jnp.arange(T)
    k_pos = jnp.arange(past + T)
    h = rmsnorm(x, lp['norm_mix'])
    u = h @ lp['w_in']
    (a_b, a_c, a_h, b_q, b_k, b_v, c_q, c_k, c_v, c_f, d_z, d_xbc, d_dt, g) = split_cols(u)
    a_out, conv_a_new = short_conv_mixer(a_b, a_c, a_h, conv_a_prev, lp['conv_a_w'])
    bk = b_k.reshape(b, T, H_B, 2 * DQ_B)
    bv = b_v.reshape(b, T, H_B, DV_B)
    b_out = diff_attention(b_q.reshape(b, T, H_B, 2 * DQ_B),
                           jnp.concatenate([dk_c.astype(bk.dtype), bk], axis=1),
                           jnp.concatenate([dv_c.astype(bv.dtype), bv], axis=1),
                           q_pos, k_pos, lp['diff_lambda'], lp['diff_subln'], li)
    ck = c_k.reshape(b, T, H_C, DH_C)
    cv = c_v.reshape(b, T, H_C, DH_C)
    logf = jax.nn.log_sigmoid(c_f.astype(jnp.float32) + lp['b_fox_f'].astype(jnp.float32))
    c_out = forgetting_attention(c_q.reshape(b, T, H_C, DH_C),
                                 jnp.concatenate([fk_c.astype(ck.dtype), ck], axis=1),
                                 jnp.concatenate([fv_c.astype(cv.dtype), cv], axis=1),
                                 jnp.concatenate([flf_c.astype(jnp.float32), logf], axis=1),
                                 q_pos, k_pos)
    d_out, sconv_new, ssm_new = mamba2_mixer(d_z, d_xbc, d_dt, sconv_prev, ssm_prev,
                                             lp['ssm_conv_w'], lp['ssm_conv_b'], lp['ssm_dt_bias'],
                                             lp['ssm_a_log'], lp['ssm_d'], lp['ssm_norm_g'])
    gates = jax.nn.sigmoid((g + lp['b_gate']).astype(jnp.float32)).astype(x.dtype)
    gates = gates.reshape(b, T, N_BRANCH, D_MODEL)
    branches = (a_out, b_out, c_out, d_out)
    merged = gates[:, :, 0] * (branches[0] @ lp['w_br'][0])
    for n in range(1, N_BRANCH):
        merged = merged + gates[:, :, n] * (branches[n] @ lp['w_br'][n])
    x = x + merged @ lp['w_out']
    hf = rmsnorm(x, lp['norm_ffn'])
    x = x + jnp.square(jax.nn.relu(hf @ lp['w_ff1'])) @ lp['w_ff2']
    new_state = (conv_a_new, bk, bv, ck, cv, logf.astype(x.dtype), sconv_new, ssm_new)
    return x, new_state


def empty_state(b, dtype):
    return (jnp.zeros((b, CONV_A - 1, W_A), dtype),
            jnp.zeros((b, 0, H_B, 2 * DQ_B), dtype),
            jnp.zeros((b, 0, H_B, DV_B), dtype),
            jnp.zeros((b, 0, H_C, DH_C), dtype),
            jnp.zeros((b, 0, H_C, DH_C), dtype),
            jnp.zeros((b, 0, H_C), dtype),
            jnp.zeros((b, CONV_D_W - 1, CONV_D), dtype),
            jnp.zeros((b, H_D, P_D, N_D), dtype))


def setup_inputs(seed: int = 0) -> dict:
    key = jax.random.key(seed)
    ks = iter(list(jax.random.split(key, 40)))
    f32 = jnp.float32

    def nrm(shape, scale=1.0):
        return jax.random.normal(next(ks), shape, f32) * scale

    def gain(shape):
        return 1.0 + nrm(shape, 0.05)

    x_prompt = nrm((BATCH, SEQ, D_MODEL))
    x_sample = nrm((DEC_BATCH, DEC_SEQ, D_MODEL))
    state_conv_a = nrm((DEPTH, DEC_BATCH, CONV_A - 1, W_A))
    cache_diff_k = nrm((DEPTH, DEC_BATCH, PAST_LEN, H_B, 2 * DQ_B))
    cache_diff_v = nrm((DEPTH, DEC_BATCH, PAST_LEN, H_B, DV_B))
    cache_fox_k = nrm((DEPTH, DEC_BATCH, PAST_LEN, H_C, DH_C))
    cache_fox_v = nrm((DEPTH, DEC_BATCH, PAST_LEN, H_C, DH_C))
    cache_fox_logf = jax.nn.log_sigmoid(FORGET_BIAS + nrm((DEPTH, DEC_BATCH, PAST_LEN, H_C)))
    state_ssm_conv = nrm((DEPTH, DEC_BATCH, CONV_D_W - 1, CONV_D))
    state_ssm = nrm((DEPTH, DEC_BATCH, H_D, P_D, N_D), 0.1)
    norm_mix_g = gain((DEPTH, D_MODEL))
    w_in = nrm((DEPTH, D_MODEL, N_IN), D_MODEL ** -0.5)
    b_fox_f = FORGET_BIAS + nrm((DEPTH, H_C), 0.1)
    b_gate = nrm((DEPTH, N_BRANCH * D_MODEL), 0.01)
    conv_a_w = nrm((DEPTH, CONV_A, W_A), CONV_A ** -0.5)
    diff_lambda = nrm((DEPTH, 4, DQ_B), 0.1)
    diff_subln_g = gain((DEPTH, DV_B))
    ssm_conv_w = nrm((DEPTH, CONV_D_W, CONV_D), CONV_D_W ** -0.5)
    ssm_conv_b = nrm((DEPTH, CONV_D), 0.01)
    dt0 = jnp.exp(jax.random.uniform(next(ks), (DEPTH, H_D), f32, math.log(1e-3), math.log(1e-1)))
    ssm_dt_bias = dt0 + jnp.log(-jnp.expm1(-dt0))
    ssm_a_log = jnp.log(jax.random.uniform(next(ks), (DEPTH, H_D), f32, 1.0, 16.0))
    ssm_d = gain((DEPTH, H_D))
    ssm_norm_g = gain((DEPTH, D_INNER))
    w_br = nrm((DEPTH, N_BRANCH, W_A, D_MODEL), W_A ** -0.5)
    w_out = nrm((DEPTH, D_MODEL, D_MODEL), D_MODEL ** -0.5)
    norm_ffn_g = gain((DEPTH, D_MODEL))
    w_ff1 = nrm((DEPTH, D_MODEL, D_FF), D_MODEL ** -0.5)
    w_ff2 = nrm((DEPTH, D_FF, D_MODEL), D_FF ** -0.5)
    norm_final_g = gain((D_MODEL,))
    return {'x_prompt': x_prompt, 'x_sample': x_sample,
            'state_conv_a': state_conv_a, 'cache_diff_k': cache_diff_k, 'cache_diff_v': cache_diff_v,
            'cache_fox_k': cache_fox_k, 'cache_fox_v': cache_fox_v, 'cache_fox_logf': cache_fox_logf,
            'state_ssm_conv': state_ssm_conv, 'state_ssm': state_ssm,
            'norm_mix_g': norm_mix_g, 'w_in': w_in, 'b_fox_f': b_fox_f, 'b_gate': b_gate,
            'conv_a_w': conv_a_w, 'diff_lambda': diff_lambda, 'diff_subln_g': diff_subln_g,
            'ssm_conv_w': ssm_conv_w, 'ssm_conv_b': ssm_conv_b, 'ssm_dt_bias': ssm_dt_bias,
            'ssm_a_log': ssm_a_log, 'ssm_d': ssm_d, 'ssm_norm_g': ssm_norm_g,
            'w_br': w_br, 'w_out': w_out, 'norm_ffn_g': norm_ffn_g, 'w_ff1': w_ff1, 'w_ff2': w_ff2,
            'norm_final_g': norm_final_g}


def reference(x_prompt, x_sample, state_conv_a, cache_diff_k, cache_diff_v, cache_fox_k, cache_fox_v,
              cache_fox_logf, state_ssm_conv, state_ssm, norm_mix_g, w_in, b_fox_f, b_gate, conv_a_w,
              diff_lambda, diff_subln_g, ssm_conv_w, ssm_conv_b, ssm_dt_bias, ssm_a_log, ssm_d,
              ssm_norm_g, w_br, w_out, norm_ffn_g, w_ff1, w_ff2, norm_final_g):
    yp, ys = x_prompt, x_sample
    p_new, s_new = [], []
    for li in range(DEPTH):
        lp = {'norm_mix': norm_mix_g[li], 'w_in': w_in[li], 'b_fox_f': b_fox_f[li], 'b_gate': b_gate[li],
              'conv_a_w': conv_a_w[li], 'diff_lambda': diff_lambda[li], 'diff_subln': diff_subln_g[li],
              'ssm_conv_w': ssm_conv_w[li], 'ssm_conv_b': ssm_conv_b[li], 'ssm_dt_bias': ssm_dt_bias[li],
              'ssm_a_log': ssm_a_log[li], 'ssm_d': ssm_d[li], 'ssm_norm_g': ssm_norm_g[li],
              'w_br': w_br[li], 'w_out': w_out[li], 'norm_ffn': norm_ffn_g[li],
              'w_ff1': w_ff1[li], 'w_ff2': w_ff2[li]}
        yp, st_p = trunk_layer(yp, li, lp, empty_state(yp.shape[0], yp.dtype))
        st_in = (state_conv_a[li], cache_diff_k[li], cache_diff_v[li], cache_fox_k[li], cache_fox_v[li],
                 cache_fox_logf[li], state_ssm_conv[li], state_ssm[li])
        ys, st_s = trunk_layer(ys, li, lp, st_in)
        p_new.append(st_p)
        s_new.append(st_s)
    y_prompt = rmsnorm(yp, norm_final_g)
    y_sample = rmsnorm(ys, norm_final_g)
    p_conv_a, p_diff_k, p_diff_v, p_fox_k, p_fox_v, p_fox_logf, p_ssm_conv, p_ssm = [jnp.stack(s) for s in zip(*p_new)]
    s_conv_a, s_diff_k, s_diff_v, s_fox_k, s_fox_v, s_fox_logf, s_ssm_conv, s_ssm = [jnp.stack(s) for s in zip(*s_new)]
    return (y_prompt, y_sample, p_conv_a, s_conv_a, p_diff_k, s_diff_k, p_diff_v, s_diff_v,
            p_fox_k, s_fox_k, p_fox_v, s_fox_v, p_fox_logf, s_fox_logf, p_ssm_conv, s_ssm_conv, p_ssm, s_ssm)
```

```python
import functools
import math

import jax
import jax.numpy as jnp
from jax import lax
from jax.experimental import pallas as pl
from jax.experimental.pallas import tpu as pltpu

F32 = jnp.float32
BF16 = jnp.bfloat16

D_MODEL = 2048
DEPTH = 2
CHUNK = 64
EPS = 1e-6
N_BRANCH = 4
W_A = 512
CONV_A = 3
H_B = 4
DQ_B = 64
DV_B = 128
H_C = 4
DH_C = 128
D_INNER = 512
P_D = 64
H_D = 8
G_D = 2
N_D = 128
CONV_D_W = 4
CONV_D = 1024
D_FF = 4 * D_MODEL

LANES = 128
VMEM_LIMIT_BYTES = 56 * 1024 * 1024

O_AB, O_AC, O_AH = 0, 512, 1024
O_BQ, O_BK, O_BV = 1536, 2048, 2560
O_CQ, O_CK, O_CV = 3072, 3584, 4096
O_DZ, O_XBC, O_MISC = 4608, 5120, 6144
MISC_W = 256
NP = O_MISC + MISC_W
MISC_CF = 0
MISC_DT = 4
S_CF, S_DZ, S_XBC, S_DT, S_G = 4608, 4612, 5124, 6148, 6156

SSD_L = 128


def _params(*sem):
    return pltpu.CompilerParams(dimension_semantics=sem, vmem_limit_bytes=VMEM_LIMIT_BYTES)


def _sigmoid(x):
    return 1.0 / (1.0 + jnp.exp(-x))


def _softplus(x):
    return jnp.maximum(x, 0.0) + jnp.log(1.0 + jnp.exp(-jnp.abs(x)))


def _log_sigmoid(x):
    return -_softplus(-x)


def _dot(a, b):
    return jnp.dot(a, b, preferred_element_type=F32)


def _dot_nt(a, b):
    return lax.dot_general(a, b, (((1,), (1,)), ((), ())), preferred_element_type=F32)


def _dot_exact(a, b):
    return jnp.dot(a, b, preferred_element_type=F32, precision=lax.Precision.HIGHEST)


def _norm_matmul_kernel(x_ref, g_ref, w_ref, o_ref, h_ref, *, relu2):
    @pl.when(pl.program_id(1) == 0)
    def _():
        x = x_ref[...]
        ms = jnp.mean(x * x, axis=-1, keepdims=True)
        h_ref[...] = (x * lax.rsqrt(ms + EPS) * g_ref[...]).astype(BF16)

    acc = _dot(h_ref[...], w_ref[...])
    if relu2:
        acc = jnp.square(jnp.maximum(acc, 0.0))
    o_ref[...] = acc.astype(o_ref.dtype)


def norm_matmul(x, g, w, *, tm, tn, relu2, out_dtype, name):
    m, k = x.shape
    n = w.shape[1]
    assert m % tm == 0 and n % tn == 0
    return pl.pallas_call(
        functools.partial(_norm_matmul_kernel, relu2=relu2),
        grid=(m // tm, n // tn),
        in_specs=[pl.BlockSpec((tm, k), lambda i, j: (i, 0)),
                  pl.BlockSpec((1, k), lambda i, j: (0, 0)),
                  pl.BlockSpec((k, tn), lambda i, j: (0, j))],
        out_specs=pl.BlockSpec((tm, tn), lambda i, j: (i, j)),
        out_shape=jax.ShapeDtypeStruct((m, n), out_dtype),
        scratch_shapes=[pltpu.VMEM((tm, k), BF16)],
        compiler_params=_params("parallel", "arbitrary"),
        name=name,
    )(x, g.reshape(1, k), w)


def _matmul_res_kernel(a_ref, w_ref, r_ref, g_ref, o_ref, acc_ref, *, nk, final_norm):
    kk = pl.program_id(1)

    @pl.when(kk == 0)
    def _():
        acc_ref[...] = jnp.zeros_like(acc_ref)

    acc_ref[...] += _dot(a_ref[...], w_ref[...])

    @pl.when(kk == nk - 1)
    def _():
        y = r_ref[...] + acc_ref[...]
        if final_norm:
            ms = jnp.mean(y * y, axis=-1, keepdims=True)
            y = y * lax.rsqrt(ms + EPS) * g_ref[...]
        o_ref[...] = y


def matmul_res(a, w, res, g, *, tm, tk, final_norm, name):
    m, k = a.shape
    n = w.shape[1]
    assert m % tm == 0 and k % tk == 0
    nk = k // tk
    return pl.pallas_call(
        functools.partial(_matmul_res_kernel, nk=nk, final_norm=final_norm),
        grid=(m // tm, nk),
        in_specs=[pl.BlockSpec((tm, tk), lambda i, kk: (i, kk)),
                  pl.BlockSpec((tk, n), lambda i, kk: (kk, 0)),
                  pl.BlockSpec((tm, n), lambda i, kk: (i, 0)),
                  pl.BlockSpec((1, n), lambda i, kk: (0, 0))],
        out_specs=pl.BlockSpec((tm, n), lambda i, kk: (i, 0)),
        out_shape=jax.ShapeDtypeStruct((m, n), F32),
        scratch_shapes=[pltpu.VMEM((tm, n), F32)],
        compiler_params=_params("parallel", "arbitrary"),
        name=name,
    )(a, w, res, g.reshape(1, n))


def _merge_kernel(x_ref, g_ref, wg_ref, bg_ref, br_ref, wbr_ref, o_ref, h_ref, acc_ref):
    jn = pl.program_id(1)
    n = pl.program_id(2)

    @pl.when((jn == 0) & (n == 0))
    def _():
        x = x_ref[...]
        ms = jnp.mean(x * x, axis=-1, keepdims=True)
        h_ref[...] = (x * lax.rsqrt(ms + EPS) * g_ref[...]).astype(BF16)

    @pl.when(n == 0)
    def _():
        acc_ref[...] = jnp.zeros_like(acc_ref)

    gate = _sigmoid(_dot(h_ref[...], wg_ref[...]) + bg_ref[...])
    acc_ref[...] += gate * _dot(br_ref[0], wbr_ref[0])

    @pl.when(n == N_BRANCH - 1)
    def _():
        o_ref[...] = acc_ref[...].astype(o_ref.dtype)


def gated_merge(x, g, wg, bg, br, wbr, *, tm, tn, name):
    m, k = x.shape
    assert m % tm == 0 and D_MODEL % tn == 0
    nj = D_MODEL // tn
    return pl.pallas_call(
        _merge_kernel,
        grid=(m // tm, nj, N_BRANCH),
        in_specs=[pl.BlockSpec((tm, k), lambda i, j, n: (i, 0)),
                  pl.BlockSpec((1, k), lambda i, j, n: (0, 0)),
                  pl.BlockSpec((k, tn), lambda i, j, n: (0, n * nj + j)),
                  pl.BlockSpec((1, tn), lambda i, j, n: (0, n * nj + j)),
                  pl.BlockSpec((1, tm, W_A), lambda i, j, n: (n, i, 0)),
                  pl.BlockSpec((1, W_A, tn), lambda i, j, n: (n, 0, j))],
        out_specs=pl.BlockSpec((tm, tn), lambda i, j, n: (i, j)),
        out_shape=jax.ShapeDtypeStruct((m, D_MODEL), BF16),
        scratch_shapes=[pltpu.VMEM((tm, k), BF16), pltpu.VMEM((tm, tn), F32)],
        compiler_params=_params("parallel", "arbitrary", "arbitrary"),
        name=name,
    )(x, g.reshape(1, k), wg, bg.reshape(1, -1), br, wbr)


def _conv_a_kernel(ab_ref, ac_ref, ah_ref, prev_ref, w_ref, o_ref, st_ref, buf_ref, *, tt, nt):
    t = pl.program_id(1)
    km1 = CONV_A - 1

    @pl.when(t == 0)
    def _():
        buf_ref[8 - km1:8, :] = prev_ref[0]

    buf_ref[8:8 + tt, :] = ac_ref[...] * ah_ref[...]
    y = buf_ref[8 - km1:8 - km1 + tt, :] * w_ref[0:1, :]
    for k in range(1, CONV_A):
        y = y + buf_ref[8 - km1 + k:8 - km1 + k + tt, :] * w_ref[k:k + 1, :]
    o_ref[...] = (ab_ref[...] * y).astype(o_ref.dtype)
    tail = buf_ref[8 + tt - km1:8 + tt, :]
    buf_ref[8 - km1:8, :] = tail

    @pl.when(t == nt - 1)
    def _():
        st_ref[0] = tail


def conv_a_mixer(u, prev, w, *, base, nb, t_len, tt, name):
    assert t_len % tt == 0 and base % tt == 0 and tt >= CONV_A - 1
    nt = t_len // tt
    rb = base // tt

    def rows(col):
        return pl.BlockSpec((tt, W_A), lambda b, t: (rb + b * nt + t, col))

    return pl.pallas_call(
        functools.partial(_conv_a_kernel, tt=tt, nt=nt),
        grid=(nb, nt),
        in_specs=[rows(O_AB // W_A), rows(O_AC // W_A), rows(O_AH // W_A),
                  pl.BlockSpec((1, CONV_A - 1, W_A), lambda b, t: (b, 0, 0)),
                  pl.BlockSpec((CONV_A, W_A), lambda b, t: (0, 0))],
        out_specs=[pl.BlockSpec((tt, W_A), lambda b, t: (b * nt + t, 0)),
                   pl.BlockSpec((1, CONV_A - 1, W_A), lambda b, t: (b, 0, 0))],
        out_shape=[jax.ShapeDtypeStruct((nb * t_len, W_A), BF16),
                   jax.ShapeDtypeStruct((nb, CONV_A - 1, W_A), F32)],
        scratch_shapes=[pltpu.VMEM((8 + tt, W_A), F32)],
        compiler_params=_params("parallel", "arbitrary"),
        name=name,
    )(u, u, u, prev, w)


def _cum_kernel(x_ref, bias_ref, init_ref, lf_ref, cum_ref, *, logsig, nchunk, valid):
    x = x_ref[0]
    r = x.shape[0]
    if logsig:
        x = _log_sigmoid(x + bias_ref[...])
    if valid < LANES:
        lane = lax.broadcasted_iota(jnp.int32, x.shape, 1)
        x = jnp.where(lane < valid, x, 0.0)
    ri = lax.broadcasted_iota(jnp.int32, (LANES, LANES), 0)
    ci = lax.broadcasted_iota(jnp.int32, (LANES, LANES), 1)
    upper = jnp.where(ri <= ci, 1.0, 0.0).astype(F32)
    within = _dot_exact(x, upper)
    cum = within + init_ref[0]
    if nchunk > 1:
        rr = lax.broadcasted_iota(jnp.int32, (r, r), 0)
        rc = lax.broadcasted_iota(jnp.int32, (r, r), 1)
        earlier = jnp.where((rc < rr) & (rc // nchunk == rr // nchunk), 1.0, 0.0).astype(F32)
        pref = _dot_exact(earlier, within)
        cum = cum + pref[:, LANES - 1:LANES]
    lf_ref[0] = x
    cum_ref[0] = cum


def cum_rows(x, bias, init, *, logsig, nchunk, valid, name):
    nb, r, _ = x.shape
    blk = pl.BlockSpec((1, r, LANES), lambda b: (b, 0, 0))
    return pl.pallas_call(
        functools.partial(_cum_kernel, logsig=logsig, nchunk=nchunk, valid=valid),
        grid=(nb,),
        in_specs=[blk, pl.BlockSpec((r, LANES), lambda b: (0, 0)), blk],
        out_specs=[blk, blk],
        out_shape=[jax.ShapeDtypeStruct(x.shape, F32)] * 2,
        compiler_params=_params("parallel"),
        name=name,
    )(x, bias, init)


def _diff_lambda(lam_ref, lam_init):
    lp = lam_ref[...]
    s1 = jnp.sum(lp[0:1, :] * lp[1:2, :], axis=-1, keepdims=True)
    s2 = jnp.sum(lp[2:3, :] * lp[3:4, :], axis=-1, keepdims=True)
    return jnp.exp(s1) - jnp.exp(s2) + lam_init


def _split_q(q):
    lane = lax.broadcasted_iota(jnp.int32, q.shape, 1)
    q1 = jnp.where(lane < DQ_B, q, 0.0)
    q2 = jnp.where(lane >= DQ_B, q, 0.0)
    return jnp.concatenate([q1, q2], axis=0).astype(BF16)


def _diff_finish(o, lam, g, lam_init, t):
    d = o[:t] - lam * o[t:]
    ms = jnp.mean(d * d, axis=-1, keepdims=True)
    return d * lax.rsqrt(ms + EPS) * g * (1.0 - lam_init)


def _flash_step(qb, kt, vt, bias, mask, scale, m_ref, l_ref, acc_ref):
    s = _dot_nt(qb, kt) * scale
    if bias is not None:
        s = s + bias
    if mask is not None:
        s = jnp.where(mask, s, -jnp.inf)
    m_old = m_ref[...]
    m_new = jnp.maximum(m_old, jnp.max(s, axis=-1, keepdims=True))
    alpha = jnp.exp(m_old - m_new)
    p = jnp.exp(s - m_new)
    l_ref[...] = alpha * l_ref[...] + jnp.sum(p, axis=-1, keepdims=True)
    acc_ref[...] = alpha * acc_ref[...] + _dot(p.astype(BF16), vt)
    m_ref[...] = m_new


def _diff_prompt_kernel(q_ref, k_ref, v_ref, lam_ref, g_ref, o_ref,
                        kb_ref, vb_ref, m_ref, l_ref, acc_ref, *, tq, lam_init):
    qi = pl.program_id(2)
    scale = DQ_B ** -0.5

    @pl.when(qi == 0)
    def _():
        kb_ref[...] = k_ref[...].astype(BF16)
        vb_ref[...] = v_ref[...].astype(BF16)

    qb = _split_q(q_ref[...])
    m_ref[...] = jnp.full_like(m_ref, -jnp.inf)
    l_ref[...] = jnp.zeros_like(l_ref)
    acc_ref[...] = jnp.zeros_like(acc_ref)

    def tiles(kj):
        off = pl.multiple_of(kj * tq, tq)
        return kb_ref[pl.ds(off, tq), :], vb_ref[pl.ds(off, tq), :]

    def body(kj, carry):
        kt, vt = tiles(kj)
        _flash_step(qb, kt, vt, None, None, scale, m_ref, l_ref, acc_ref)
        return carry

    lax.fori_loop(0, qi, body, 0)
    kt, vt = tiles(qi)
    row = lax.broadcasted_iota(jnp.int32, (2 * tq, tq), 0)
    col = lax.broadcasted_iota(jnp.int32, (2 * tq, tq), 1)
    row = jnp.where(row >= tq, row - tq, row)
    mask = (col // CHUNK) <= (row // CHUNK)
    _flash_step(qb, kt, vt, None, mask, scale, m_ref, l_ref, acc_ref)

    o = acc_ref[...] / l_ref[...]
    lam = _diff_lambda(lam_ref, lam_init)
    o_ref[...] = _diff_finish(o, lam, g_ref[...], lam_init, tq).astype(o_ref.dtype)


def diff_attention_prompt(u, lam_p, g, *, nb, t_len, tq, lam_init, name):
    assert t_len % tq == 0 and tq % CHUNK == 0
    nq = t_len // tq
    hw = 2 * DQ_B
    return pl.pallas_call(
        functools.partial(_diff_prompt_kernel, tq=tq, lam_init=lam_init),
        grid=(nb, H_B, nq),
        in_specs=[pl.BlockSpec((tq, hw), lambda b, h, i: (b * nq + i, O_BQ // hw + h)),
                  pl.BlockSpec((t_len, hw), lambda b, h, i: (b, O_BK // hw + h)),
                  pl.BlockSpec((t_len, DV_B), lambda b, h, i: (b, O_BV // DV_B + h)),
                  pl.BlockSpec((4, DQ_B), lambda b, h, i: (0, 0)),
                  pl.BlockSpec((1, DV_B), lambda b, h, i: (0, 0))],
        out_specs=pl.BlockSpec((tq, DV_B), lambda b, h, i: (b * nq + i, h)),
        out_shape=jax.ShapeDtypeStruct((nb * t_len, H_B * DV_B), BF16),
        scratch_shapes=[pltpu.VMEM((t_len, hw), BF16), pltpu.VMEM((t_len, DV_B), BF16),
                        pltpu.VMEM((2 * tq, 1), F32), pltpu.VMEM((2 * tq, 1), F32),
                        pltpu.VMEM((2 * tq, DV_B), F32)],
        compiler_params=_params("parallel", "parallel", "arbitrary"),
        name=name,
    )(u, u, u, lam_p, g.reshape(1, DV_B))


def _fox_prompt_kernel(q_ref, k_ref, v_ref, qc_ref, kc_ref, o_ref,
                       kb_ref, vb_ref, m_ref, l_ref, acc_ref, *, tq):
    h = pl.program_id(1)
    qi = pl.program_id(2)
    scale = DH_C ** -0.5

    @pl.when(qi == 0)
    def _():
        kb_ref[...] = k_ref[...].astype(BF16)
        vb_ref[...] = v_ref[...].astype(BF16)

    qb = q_ref[...].astype(BF16)
    qc_all = qc_ref[...]
    lane = lax.broadcasted_iota(jnp.int32, qc_all.shape, 1)
    qc = jnp.sum(jnp.where(lane == h, qc_all, 0.0), axis=-1, keepdims=True)
    m_ref[...] = jnp.full_like(m_ref, -jnp.inf)
    l_ref[...] = jnp.zeros_like(l_ref)
    acc_ref[...] = jnp.zeros_like(acc_ref)

    def tiles(kj):
        off = pl.multiple_of(kj * tq, tq)
        return kb_ref[pl.ds(off, tq), :], vb_ref[pl.ds(off, tq), :], kc_ref[0, pl.ds(kj, 1), :]

    def body(kj, carry):
        kt, vt, kc = tiles(kj)
        _flash_step(qb, kt, vt, qc - kc, None, scale, m_ref, l_ref, acc_ref)
        return carry

    lax.fori_loop(0, qi, body, 0)
    kt, vt, kc = tiles(qi)
    row = lax.broadcasted_iota(jnp.int32, (tq, tq), 0)
    col = lax.broadcasted_iota(jnp.int32, (tq, tq), 1)
    _flash_step(qb, kt, vt, qc - kc, col <= row, scale, m_ref, l_ref, acc_ref)
    o_ref[...] = (acc_ref[...] / l_ref[...]).astype(o_ref.dtype)


def fox_attention_prompt(u, qcum, kcum, *, nb, t_len, tq, name):
    assert t_len % tq == 0
    nq = t_len // tq
    return pl.pallas_call(
        functools.partial(_fox_prompt_kernel, tq=tq),
        grid=(nb, H_C, nq),
        in_specs=[pl.BlockSpec((tq, DH_C), lambda b, h, i: (b * nq + i, O_CQ // DH_C + h)),
                  pl.BlockSpec((t_len, DH_C), lambda b, h, i: (b, O_CK // DH_C + h)),
                  pl.BlockSpec((t_len, DH_C), lambda b, h, i: (b, O_CV // DH_C + h)),
                  pl.BlockSpec((tq, H_C), lambda b, h, i: (b * nq + i, 0)),
                  pl.BlockSpec((1, nq, tq), lambda b, h, i: (b * H_C + h, 0, 0))],
        out_specs=pl.BlockSpec((tq, DH_C), lambda b, h, i: (b * nq + i, h)),
        out_shape=jax.ShapeDtypeStruct((nb * t_len, H_C * DH_C), BF16),
        scratch_shapes=[pltpu.VMEM((t_len, DH_C), BF16), pltpu.VMEM((t_len, DH_C), BF16),
                        pltpu.VMEM((tq, 1), F32), pltpu.VMEM((tq, 1), F32),
                        pltpu.VMEM((tq, DH_C), F32)],
        compiler_params=_params("parallel", "parallel", "arbitrary"),
        name=name,
    )(u, u, u, qcum, kcum)


def _two_part_softmax_pv(s_c, s_n, vc, vn):
    m = jnp.maximum(jnp.max(s_c, axis=-1, keepdims=True), jnp.max(s_n, axis=-1, keepdims=True))
    p_c = jnp.exp(s_c - m)
    p_n = jnp.exp(s_n - m)
    l = jnp.sum(p_c, axis=-1, keepdims=True) + jnp.sum(p_n, axis=-1, keepdims=True)
    o = _dot(p_c.astype(BF16), vc) + _dot(p_n.astype(BF16), vn)
    return o / l


def _diff_sample_kernel(q_ref, kn_ref, vn_ref, kc_ref, vc_ref, lam_ref, g_ref, o_ref,
                        *, ts, past, lam_init):
    scale = DQ_B ** -0.5
    lam = _diff_lambda(lam_ref, lam_init)
    row = lax.broadcasted_iota(jnp.int32, (2 * ts, ts), 0)
    col = lax.broadcasted_iota(jnp.int32, (2 * ts, ts), 1)
    row = jnp.where(row >= ts, row - ts, row)
    mask = ((past + col) // CHUNK) <= ((past + row) // CHUNK)
    for h in range(H_B):
        sl = slice(h * DV_B, (h + 1) * DV_B)
        qb = _split_q(q_ref[:, sl])
        kc = kc_ref[0, :, sl].astype(BF16)
        vc = vc_ref[0, :, sl].astype(BF16)
        kn = kn_ref[:, sl].astype(BF16)
        vn = vn_ref[:, sl].astype(BF16)
        s_c = _dot_nt(qb, kc) * scale
        s_n = jnp.where(mask, _dot_nt(qb, kn) * scale, -jnp.inf)
        o = _two_part_softmax_pv(s_c, s_n, vc, vn)
        o_ref[:, sl] = _diff_finish(o, lam, g_ref[...], lam_init, ts).astype(o_ref.dtype)


def diff_attention_sample(u, kc, vc, lam_p, g, *, base, nb, ts, lam_init, name):
    past = kc.shape[1]
    assert (past - 1) // CHUNK <= past // CHUNK and base % ts == 0
    rb = base // ts
    wide = H_B * DV_B
    new = lambda col: pl.BlockSpec((ts, wide), lambda b: (rb + b, col))
    cache = pl.BlockSpec((1, past, wide), lambda b: (b, 0, 0))
    return pl.pallas_call(
        functools.partial(_diff_sample_kernel, ts=ts, past=past, lam_init=lam_init),
        grid=(nb,),
        in_specs=[new(O_BQ // wide), new(O_BK // wide), new(O_BV // wide), cache, cache,
                  pl.BlockSpec((4, DQ_B), lambda b: (0, 0)),
                  pl.BlockSpec((1, DV_B), lambda b: (0, 0))],
        out_specs=pl.BlockSpec((ts, wide), lambda b: (b, 0)),
        out_shape=jax.ShapeDtypeStruct((nb * ts, wide), BF16),
        compiler_params=_params("parallel"),
        name=name,
    )(u, u, u, kc, vc, lam_p, g.reshape(1, DV_B))


def _fox_sample_kernel(q_ref, kn_ref, vn_ref, kc_ref, vc_ref, qcum_ref, kcum_c_ref, kcum_n_ref,
                       o_ref, *, ts):
    scale = DH_C ** -0.5
    row = lax.broadcasted_iota(jnp.int32, (ts, ts), 0)
    col = lax.broadcasted_iota(jnp.int32, (ts, ts), 1)
    mask = col <= row
    for h in range(H_C):
        sl = slice(h * DH_C, (h + 1) * DH_C)
        qb = q_ref[:, sl].astype(BF16)
        kc = kc_ref[0, :, sl].astype(BF16)
        vc = vc_ref[0, :, sl].astype(BF16)
        kn = kn_ref[:, sl].astype(BF16)
        vn = vn_ref[:, sl].astype(BF16)
        qc = qcum_ref[0, :, h:h + 1]
        kcc = kcum_c_ref[0, h:h + 1, :]
        kcn = kcum_n_ref[0, h:h + 1, 0:ts]
        s_c = _dot_nt(qb, kc) * scale + (qc - kcc)
        s_n = jnp.where(mask, _dot_nt(qb, kn) * scale + (qc - kcn), -jnp.inf)
        o_ref[:, sl] = _two_part_softmax_pv(s_c, s_n, vc, vn).astype(o_ref.dtype)


def fox_attention_sample(u, kc, vc, qcum, kcum_c, kcum_n, *, base, nb, ts, name):
    past = kc.shape[1]
    assert base % ts == 0
    rb = base // ts
    wide = H_C * DH_C
    new = lambda col: pl.BlockSpec((ts, wide), lambda b: (rb + b, col))
    cache = pl.BlockSpec((1, past, wide), lambda b: (b, 0, 0))
    return pl.pallas_call(
        functools.partial(_fox_sample_kernel, ts=ts),
        grid=(nb,),
        in_specs=[new(O_CQ // wide), new(O_CK // wide), new(O_CV // wide), cache, cache,
                  pl.BlockSpec((1, ts, H_C), lambda b: (b, 0, 0)),
                  pl.BlockSpec((1, H_C, past), lambda b: (b, 0, 0)),
                  pl.BlockSpec((1, 8, LANES), lambda b: (b, 0, 0))],
        out_specs=pl.BlockSpec((ts, wide), lambda b: (b, 0)),
        out_shape=jax.ShapeDtypeStruct((nb * ts, wide), BF16),
        compiler_params=_params("parallel"),
        name=name,
    )(u, u, u, kc, vc, qcum, kcum_c, kcum_n)


def _pair(v, j):
    lane = lax.broadcasted_iota(jnp.int32, (v.shape[0], 2 * P_D), 1)
    c0 = MISC_DT + 2 * j
    return jnp.where(lane < P_D, v[:, c0:c0 + 1], v[:, c0 + 1:c0 + 2])


def _ssd_kernel(xbc_ref, z_ref, misc_ref, cprev_ref, sprev_ref, cw_ref, cb_ref, dtb_ref, alog_ref,
                dskip_ref, ng_ref, o_ref, sout_ref, buf_ref, st_ref, *, tb, nc):
    c = pl.program_id(1)
    L = SSD_L
    km1 = CONV_D_W - 1

    @pl.when(c == 0)
    def _():
        buf_ref[8 - km1:8, :] = cprev_ref[0]
        st_ref[...] = sprev_ref[0]

    if tb < L:
        buf_ref[8 + tb:8 + L, :] = jnp.zeros((L - tb, CONV_D), F32)
    buf_ref[8:8 + tb, :] = xbc_ref[...]
    conv = buf_ref[8 - km1:8 - km1 + L, :] * cw_ref[0:1, :]
    for k in range(1, CONV_D_W):
        conv = conv + buf_ref[8 - km1 + k:8 - km1 + k + L, :] * cw_ref[k:k + 1, :]
    tail = buf_ref[8 + tb - km1:8 + tb, :]
    buf_ref[8 - km1:8, :] = tail
    conv = conv + cb_ref[...]
    act = conv * _sigmoid(conv)
    xs = act[:, :D_INNER]
    bm = act[:, D_INNER:D_INNER + G_D * N_D].astype(BF16)
    cm = act[:, D_INNER + G_D * N_D:].astype(BF16)

    if tb < L:
        misc = jnp.concatenate([misc_ref[:, 0:LANES], jnp.zeros((L - tb, LANES), F32)], axis=0)
        zz = jnp.concatenate([z_ref[...], jnp.zeros((L - tb, D_INNER), F32)], axis=0)
    else:
        misc = misc_ref[:, 0:LANES]
        zz = z_ref[...]
    dt = _softplus(misc + dtb_ref[...])
    if tb < L:
        rowi = lax.broadcasted_iota(jnp.int32, dt.shape, 0)
        dt = jnp.where(rowi < tb, dt, 0.0)
    da = dt * (-jnp.exp(alog_ref[...]))
    ri = lax.broadcasted_iota(jnp.int32, (L, L), 0)
    ci = lax.broadcasted_iota(jnp.int32, (L, L), 1)
    causal = ci <= ri
    a_cum = _dot_exact(jnp.where(causal, 1.0, 0.0).astype(F32), da)
    a_cum_t = a_cum.T
    a_last = a_cum[L - 1:L, :]
    e_cum = jnp.exp(a_cum)
    e_end = jnp.exp(a_last - a_cum)
    e_last = jnp.exp(a_last)
    lane2 = lax.broadcasted_iota(jnp.int32, (L, 2 * P_D), 1)
    rows2 = lax.broadcasted_iota(jnp.int32, (2 * P_D, N_D), 0)

    ys = []
    for j in range(H_D // 2):
        g = (2 * j) // (H_D // G_D)
        bg = bm[:, g * N_D:(g + 1) * N_D]
        cg = cm[:, g * N_D:(g + 1) * N_D]
        cb = _dot_nt(cg, bg)
        x_pair = xs[:, j * 2 * P_D:(j + 1) * 2 * P_D]
        xdt = x_pair * _pair(dt, j)
        y = jnp.zeros((L, 2 * P_D), F32)
        for e in range(2):
            hl = MISC_DT + 2 * j + e
            seg = a_cum[:, hl:hl + 1] - a_cum_t[hl:hl + 1, :]
            decay = jnp.exp(jnp.where(causal, seg, -jnp.inf))
            half = (lane2 < P_D) if e == 0 else (lane2 >= P_D)
            xm = jnp.where(half, xdt, 0.0).astype(BF16)
            y = y + _dot((cb * decay).astype(BF16), xm)
        s_pair = st_ref[j * 2 * P_D:(j + 1) * 2 * P_D, :]
        y = y + _dot_nt(cg, s_pair.astype(BF16)) * _pair(e_cum, j)
        y = y + dskip_ref[:, j * 2 * P_D:(j + 1) * 2 * P_D] * x_pair
        ys.append(y)
        xw = xdt * _pair(e_end, j)
        c0 = MISC_DT + 2 * j
        keep = jnp.where(rows2 < P_D, e_last[:, c0:c0 + 1], e_last[:, c0 + 1:c0 + 2])
        st_ref[j * 2 * P_D:(j + 1) * 2 * P_D, :] = s_pair * keep + _dot(xw.T.astype(BF16), bg)

    y = jnp.concatenate(ys, axis=1)
    gated = y * (zz * _sigmoid(zz))
    gw = D_INNER // G_D
    outs = []
    for g in range(G_D):
        gg = gated[:, g * gw:(g + 1) * gw]
        ms = jnp.mean(gg * gg, axis=-1, keepdims=True)
        outs.append(gg * lax.rsqrt(ms + EPS))
    out = jnp.concatenate(outs, axis=1) * ng_ref[...]
    o_ref[...] = out[0:tb, :].astype(o_ref.dtype)

    @pl.when(c == nc - 1)
    def _():
        sout_ref[0] = st_ref[...]


def mamba2_mixer(u, conv_prev, ssm_prev, cw, cb, dtb, alog, dskip, ng, *, base, nb, t_len, tb, name):
    assert t_len % tb == 0 and base % tb == 0 and tb <= SSD_L and tb >= CONV_D_W - 1
    assert t_len == tb or tb == SSD_L
    nc = t_len // tb
    rb = base // tb
    rows = lambda w, col: pl.BlockSpec((tb, w), lambda b, c: (rb + b * nc + c, col))
    full = lambda shape: pl.BlockSpec(shape, lambda b, c: tuple(0 for _ in shape))
    return pl.pallas_call(
        functools.partial(_ssd_kernel, tb=tb, nc=nc),
        grid=(nb, nc),
        in_specs=[rows(CONV_D, O_XBC // CONV_D), rows(D_INNER, O_DZ // D_INNER),
                  rows(MISC_W, O_MISC // MISC_W),
                  pl.BlockSpec((1, CONV_D_W - 1, CONV_D), lambda b, c: (b, 0, 0)),
                  pl.BlockSpec((1, H_D * P_D, N_D), lambda b, c: (b, 0, 0)),
                  full((CONV_D_W, CONV_D)), full((1, CONV_D)), full((1, LANES)), full((1, LANES)),
                  full((1, D_INNER)), full((1, D_INNER))],
        out_specs=[pl.BlockSpec((tb, D_INNER), lambda b, c: (b * nc + c, 0)),
                   pl.BlockSpec((1, H_D * P_D, N_D), lambda b, c: (b, 0, 0))],
        out_shape=[jax.ShapeDtypeStruct((nb * t_len, D_INNER), BF16),
                   jax.ShapeDtypeStruct((nb, H_D * P_D, N_D), F32)],
        scratch_shapes=[pltpu.VMEM((8 + SSD_L, CONV_D), F32), pltpu.VMEM((H_D * P_D, N_D), F32)],
        compiler_params=_params("parallel", "arbitrary"),
        name=name,
    )(u, u, u, conv_prev, ssm_prev, cw, cb, dtb, alog, dskip, ng)


def _pack_w_in(w):
    pad = jnp.zeros((w.shape[0], MISC_W - H_C - H_D), w.dtype)
    return jnp.concatenate([w[:, :S_CF], w[:, S_DZ:S_XBC], w[:, S_XBC:S_DT],
                            w[:, S_CF:S_DZ], w[:, S_DT:S_G], pad], axis=1).astype(BF16)


def _head_lanes(v):
    return jnp.zeros((1, LANES), F32).at[0, MISC_DT:MISC_DT + H_D].set(v.astype(F32))


def kernel(x_prompt, x_sample, state_conv_a, cache_diff_k, cache_diff_v, cache_fox_k, cache_fox_v,
           cache_fox_logf, state_ssm_conv, state_ssm, norm_mix_g, w_in, b_fox_f, b_gate, conv_a_w,
           diff_lambda, diff_subln_g, ssm_conv_w, ssm_conv_b, ssm_dt_bias, ssm_a_log, ssm_d,
           ssm_norm_g, w_br, w_out, norm_ffn_g, w_ff1, w_ff2, norm_final_g):
    pb, pt, _ = x_prompt.shape
    sb, st, _ = x_sample.shape
    past = cache_diff_k.shape[2]
    mp, ms = pb * pt, sb * st
    tq = 256
    nq = pt // tq
    x = jnp.concatenate([x_prompt.reshape(mp, D_MODEL), x_sample.reshape(ms, D_MODEL)], axis=0)
    tm = 768
    assert (mp + ms) % tm == 0

    outs_p = [[] for _ in range(8)]
    outs_s = [[] for _ in range(8)]
    for li in range(DEPTH):
        lam_init = 0.8 - 0.6 * math.exp(-0.3 * li)
        u = norm_matmul(x, norm_mix_g[li], _pack_w_in(w_in[li]), tm=tm, tn=1280, relu2=False,
                        out_dtype=F32, name=f"in_proj_{li}")
        up = u[:mp].reshape(pb, pt, NP)
        us = u[mp:].reshape(sb, st, NP)

        a_p, ca_p = conv_a_mixer(u, jnp.zeros((pb, CONV_A - 1, W_A), F32), conv_a_w[li],
                                 base=0, nb=pb, t_len=pt, tt=512, name=f"conv_a_p_{li}")
        a_s, ca_s = conv_a_mixer(u, state_conv_a[li], conv_a_w[li],
                                 base=mp, nb=sb, t_len=st, tt=st, name=f"conv_a_s_{li}")

        b_p = diff_attention_prompt(u, diff_lambda[li], diff_subln_g[li], nb=pb, t_len=pt, tq=tq,
                                    lam_init=lam_init, name=f"diff_p_{li}")
        b_s = diff_attention_sample(u, cache_diff_k[li].reshape(sb, past, H_B * 2 * DQ_B),
                                    cache_diff_v[li].reshape(sb, past, H_B * DV_B),
                                    diff_lambda[li], diff_subln_g[li], base=mp, nb=sb, ts=st,
                                    lam_init=lam_init, name=f"diff_s_{li}")

        ncp = pt // LANES
        bias_p = jnp.broadcast_to(jnp.repeat(b_fox_f[li], ncp)[:, None], (H_C * ncp, LANES))
        cf_p = jnp.transpose(up[:, :, O_MISC + MISC_CF:O_MISC + MISC_CF + H_C], (0, 2, 1))
        lf_p, cum_p = cum_rows(cf_p.reshape(pb, H_C * ncp, LANES), bias_p,
                               jnp.zeros((pb, H_C * ncp, LANES), F32),
                               logsig=True, nchunk=ncp, valid=LANES, name=f"cum_p_{li}")
        lf_p = lf_p.reshape(pb, H_C, pt)
        cum_p = cum_p.reshape(pb, H_C, pt)
        c_p = fox_attention_prompt(u, jnp.transpose(cum_p, (0, 2, 1)).reshape(mp, H_C),
                                   cum_p.reshape(pb * H_C, nq, tq), nb=pb, t_len=pt, tq=tq,
                                   name=f"fox_p_{li}")
        ncc = past // LANES
        rc = H_C * ncc
        lfc = jnp.transpose(cache_fox_logf[li], (0, 2, 1)).reshape(sb, rc, LANES)
        lfc = jnp.pad(lfc, ((0, 0), (0, LANES - rc), (0, 0)))
        _, cum_c = cum_rows(lfc, jnp.zeros((LANES, LANES), F32), jnp.zeros((sb, LANES, LANES), F32),
                            logsig=False, nchunk=ncc, valid=LANES, name=f"cum_c_{li}")
        cum_c = cum_c[:, :rc].reshape(sb, H_C, past)
        cf_s = jnp.transpose(us[:, :, O_MISC + MISC_CF:O_MISC + MISC_CF + H_C], (0, 2, 1))
        cf_s = jnp.pad(cf_s, ((0, 0), (0, 8 - H_C), (0, LANES - st)))
        bias_s = jnp.broadcast_to(jnp.pad(b_fox_f[li], (0, 8 - H_C))[:, None], (8, LANES))
        init_s = jnp.broadcast_to(jnp.pad(cum_c[:, :, -1], ((0, 0), (0, 8 - H_C)))[:, :, None],
                                  (sb, 8, LANES))
        lf_s, cum_s = cum_rows(cf_s, bias_s, init_s, logsig=True, nchunk=1, valid=st,
                               name=f"cum_s_{li}")
        c_s = fox_attention_sample(u, cache_fox_k[li].reshape(sb, past, H_C * DH_C),
                                   cache_fox_v[li].reshape(sb, past, H_C * DH_C),
                                   jnp.transpose(cum_s[:, :H_C, :st], (0, 2, 1)), cum_c, cum_s,
                                   base=mp, nb=sb, ts=st, name=f"fox_s_{li}")

        dtb = _head_lanes(ssm_dt_bias[li])
        alog = _head_lanes(ssm_a_log[li])
        dskip = jnp.repeat(ssm_d[li].astype(F32), P_D)[None, :]
        ssm_args = (ssm_conv_w[li], ssm_conv_b[li][None, :], dtb, alog, dskip, ssm_norm_g[li][None, :])
        d_p, ssm_p = mamba2_mixer(u, jnp.zeros((pb, CONV_D_W - 1, CONV_D), F32),
                                  jnp.zeros((pb, H_D * P_D, N_D), F32), *ssm_args,
                                  base=0, nb=pb, t_len=pt, tb=SSD_L, name=f"ssd_p_{li}")
        d_s, ssm_s = mamba2_mixer(u, state_ssm_conv[li], state_ssm[li].reshape(sb, H_D * P_D, N_D),
                                  *ssm_args, base=mp, nb=sb, t_len=st, tb=st, name=f"ssd_s_{li}")

        br = jnp.stack([jnp.concatenate([p, s], axis=0)
                        for p, s in ((a_p, a_s), (b_p, b_s), (c_p, c_s), (d_p, d_s))])
        merged = gated_merge(x, norm_mix_g[li], w_in[li][:, S_G:].astype(BF16), b_gate[li], br,
                             w_br[li].astype(BF16), tm=tm, tn=1024, name=f"merge_{li}")
        x = matmul_res(merged, w_out[li].astype(BF16), x, norm_final_g, tm=tm, tk=1024,
                       final_norm=False, name=f"out_proj_{li}")
        hid = norm_matmul(x, norm_ffn_g[li], w_ff1[li].astype(BF16), tm=tm, tn=1024, relu2=True,
                          out_dtype=BF16, name=f"ff1_{li}")
        x = matmul_res(hid, w_ff2[li].astype(BF16), x, norm_final_g, tm=tm, tk=1024,
                       final_norm=(li == DEPTH - 1), name=f"ff2_{li}")

        km1 = CONV_D_W - 1
        outs_p[0].append(ca_p)
        outs_s[0].append(ca_s)
        for idx, (off, hh, dd) in enumerate(((O_BK, H_B, 2 * DQ_B), (O_BV, H_B, DV_B),
                                             (O_CK, H_C, DH_C), (O_CV, H_C, DH_C)), start=1):
            outs_p[idx].append(up[:, :, off:off + hh * dd].reshape(pb, pt, hh, dd))
            outs_s[idx].append(us[:, :, off:off + hh * dd].reshape(sb, st, hh, dd))
        outs_p[5].append(jnp.transpose(lf_p, (0, 2, 1)))
        outs_s[5].append(jnp.transpose(lf_s[:, :H_C, :st], (0, 2, 1)))
        outs_p[6].append(up[:, pt - km1:, O_XBC:O_XBC + CONV_D])
        outs_s[6].append(jnp.concatenate([state_ssm_conv[li], us[:, :, O_XBC:O_XBC + CONV_D]],
                                         axis=1)[:, st:])
        outs_p[7].append(ssm_p.reshape(pb, H_D, P_D, N_D))
        outs_s[7].append(ssm_s.reshape(sb, H_D, P_D, N_D))

    y_prompt = x[:mp].reshape(pb, pt, D_MODEL)
    y_sample = x[mp:].reshape(sb, st, D_MODEL)
    res = [y_prompt, y_sample]
    for idx in range(8):
        res.append(jnp.stack(outs_p[idx]))
        res.append(jnp.stack(outs_s[idx]))
    return tuple(res)
```

```python
import functools
import math

import jax
import jax.numpy as jnp
from jax import lax
from jax.experimental import pallas as pl
from jax.experimental.pallas import tpu as pltpu

F32 = jnp.float32
BF16 = jnp.bfloat16

D_MODEL = 2048
DEPTH = 2
CHUNK = 64
EPS = 1e-6
N_BRANCH = 4
W_A = 512
CONV_A = 3
H_B = 4
DQ_B = 64
DV_B = 128
H_C = 4
DH_C = 128
D_INNER = 512
P_D = 64
H_D = 8
G_D = 2
N_D = 128
CONV_D_W = 4
CONV_D = 1024
D_FF = 4 * D_MODEL

LANES = 128
VMEM_LIMIT_BYTES = 56 * 1024 * 1024

O_AB, O_AC, O_AH = 0, 512, 1024
O_BQ, O_BK, O_BV = 1536, 2048, 2560
O_CQ, O_CK, O_CV = 3072, 3584, 4096
O_DZ, O_XBC, O_MISC = 4608, 5120, 6144
MISC_W = 256
NP = O_MISC + MISC_W
MISC_CF = 0
MISC_DT = 4
S_CF, S_DZ, S_XBC, S_DT, S_G = 4608, 4612, 5124, 6148, 6156
N_IN = S_G + N_BRANCH * D_MODEL

SSD_L = 128


def _params(*sem):
    return pltpu.CompilerParams(dimension_semantics=sem, vmem_limit_bytes=VMEM_LIMIT_BYTES)


def _sigmoid(x):
    return 1.0 / (1.0 + jnp.exp(-x))


def _softplus(x):
    return jnp.maximum(x, 0.0) + jnp.log(1.0 + jnp.exp(-jnp.abs(x)))


def _log_sigmoid(x):
    return -_softplus(-x)


def _dot(a, b):
    return jnp.dot(a, b, preferred_element_type=F32)


def _dot_nt(a, b):
    return lax.dot_general(a, b, (((1,), (1,)), ((), ())), preferred_element_type=F32)


def _dot_exact(a, b):
    return jnp.dot(a, b, preferred_element_type=F32, precision=lax.Precision.HIGHEST)


def _lane_tile(v, reps):
    return v if reps == 1 else jnp.concatenate([v] * reps, axis=1)


RB = 256


def _repack_kernel(tbl_ref, w_ref, wm_ref, o_ref, *, misc_block):
    del tbl_ref
    j = pl.program_id(0)
    depth, k = w_ref.shape[1], w_ref.shape[2]

    @pl.when(j != misc_block)
    def _():
        for l in range(depth):
            o_ref[l] = w_ref[:, l, :].T.astype(BF16)

    @pl.when(j == misc_block)
    def _():
        lane = lax.broadcasted_iota(jnp.int32, (k, LANES), 1)
        for l in range(depth):
            cf = w_ref[0:LANES, l, :].T
            dt = wm_ref[:, l, :].T
            misc = jnp.where(lane < MISC_CF + H_C, cf, jnp.where(lane < MISC_DT + H_D, dt, 0.0))
            o_ref[l, :, 0:LANES] = misc.astype(BF16)
            o_ref[l, :, LANES:RB] = jnp.zeros((k, RB - LANES), BF16)


def _repack(w_t, src_rows, misc_block, name):
    n, depth, k = w_t.shape
    nblk = len(src_rows)
    grid_spec = pltpu.PrefetchScalarGridSpec(
        num_scalar_prefetch=1,
        grid=(nblk,),
        in_specs=[pl.BlockSpec((pl.Element(RB), pl.Element(depth), pl.Element(k)),
                               lambda j, tbl: (tbl[j], 0, 0)),
                  pl.BlockSpec((LANES, depth, k), lambda j, tbl: ((S_DT - MISC_DT) // LANES, 0, 0))],
        out_specs=pl.BlockSpec((depth, k, RB), lambda j, tbl: (0, 0, j)),
    )
    return pl.pallas_call(
        functools.partial(_repack_kernel, misc_block=misc_block),
        grid_spec=grid_spec,
        out_shape=jax.ShapeDtypeStruct((depth, k, nblk * RB), BF16),
        compiler_params=_params("arbitrary"),
        name=name,
    )(jnp.asarray(src_rows, jnp.int32), w_t, w_t)


def repack_w_in(w_in):
    assert w_in.shape[2] == N_IN and (S_DT - MISC_DT) % LANES == 0 and MISC_W == RB
    assert MISC_CF == 0 and S_DT - MISC_DT + LANES <= N_IN
    w_t = jnp.transpose(w_in, (2, 0, 1))
    rows = list(range(0, S_CF, RB))
    rows += [S_DZ + i for i in range(0, S_XBC - S_DZ, RB)]
    rows += [S_XBC + i for i in range(0, S_DT - S_XBC, RB)]
    misc_block = len(rows)
    rows += [S_CF]
    assert len(rows) * RB == NP
    w_proj = _repack(w_t, rows, misc_block, "repack_w_proj")
    rows_g = [S_G + i for i in range(0, N_BRANCH * D_MODEL, RB)]
    w_gate = _repack(w_t, rows_g, -1, "repack_w_gate")
    return w_proj, w_gate


def _norm_matmul_kernel(x_ref, g_ref, w_ref, o_ref, h_ref, *, relu2):
    @pl.when(pl.program_id(1) == 0)
    def _():
        x = x_ref[...]
        ms = jnp.mean(x * x, axis=-1, keepdims=True)
        h_ref[...] = (x * lax.rsqrt(ms + EPS) * g_ref[...]).astype(BF16)

    acc = _dot(h_ref[...], w_ref[0])
    if relu2:
        acc = jnp.square(jnp.maximum(acc, 0.0))
    o_ref[...] = acc.astype(o_ref.dtype)


def norm_matmul(x, g, w, li, *, tm, tn, relu2, out_dtype, name):
    m, k = x.shape
    n = w.shape[2]
    assert m % tm == 0 and n % tn == 0
    return pl.pallas_call(
        functools.partial(_norm_matmul_kernel, relu2=relu2),
        grid=(m // tm, n // tn),
        in_specs=[pl.BlockSpec((tm, k), lambda i, j: (i, 0)),
                  pl.BlockSpec((1, k), lambda i, j: (0, 0)),
                  pl.BlockSpec((1, k, tn), lambda i, j: (li, 0, j))],
        out_specs=pl.BlockSpec((tm, tn), lambda i, j: (i, j)),
        out_shape=jax.ShapeDtypeStruct((m, n), out_dtype),
        scratch_shapes=[pltpu.VMEM((tm, k), BF16)],
        compiler_params=_params("parallel", "arbitrary"),
        name=name,
    )(x, g.reshape(1, k), w)


def _matmul_res_kernel(a_ref, w_ref, r_ref, g_ref, o_ref, acc_ref, *, nk, final_norm):
    kk = pl.program_id(1)

    @pl.when(kk == 0)
    def _():
        acc_ref[...] = jnp.zeros_like(acc_ref)

    acc_ref[...] += _dot(a_ref[...], w_ref[0])

    @pl.when(kk == nk - 1)
    def _():
        y = r_ref[...] + acc_ref[...]
        if final_norm:
            ms = jnp.mean(y * y, axis=-1, keepdims=True)
            y = y * lax.rsqrt(ms + EPS) * g_ref[...]
        o_ref[...] = y


def matmul_res(a, w, li, res, g, *, tm, tk, final_norm, name):
    m, k = a.shape
    n = w.shape[2]
    assert m % tm == 0 and k % tk == 0
    nk = k // tk
    return pl.pallas_call(
        functools.partial(_matmul_res_kernel, nk=nk, final_norm=final_norm),
        grid=(m // tm, nk),
        in_specs=[pl.BlockSpec((tm, tk), lambda i, kk: (i, kk)),
                  pl.BlockSpec((1, tk, n), lambda i, kk: (li, kk, 0)),
                  pl.BlockSpec((tm, n), lambda i, kk: (i, 0)),
                  pl.BlockSpec((1, n), lambda i, kk: (0, 0))],
        out_specs=pl.BlockSpec((tm, n), lambda i, kk: (i, 0)),
        out_shape=jax.ShapeDtypeStruct((m, n), F32),
        scratch_shapes=[pltpu.VMEM((tm, n), F32)],
        compiler_params=_params("parallel", "arbitrary"),
        name=name,
    )(a, w, res, g.reshape(1, n))


def _merge_kernel(x_ref, g_ref, wg_ref, bg_ref, br_ref, wbr_ref, o_ref, h_ref, acc_ref):
    jn = pl.program_id(1)
    n = pl.program_id(2)

    @pl.when((jn == 0) & (n == 0))
    def _():
        x = x_ref[...]
        ms = jnp.mean(x * x, axis=-1, keepdims=True)
        h_ref[...] = (x * lax.rsqrt(ms + EPS) * g_ref[...]).astype(BF16)

    @pl.when(n == 0)
    def _():
        acc_ref[...] = jnp.zeros_like(acc_ref)

    gate = _sigmoid(_dot(h_ref[...], wg_ref[0]) + bg_ref[...])
    acc_ref[...] += gate * _dot(br_ref[0], wbr_ref[0, 0])

    @pl.when(n == N_BRANCH - 1)
    def _():
        o_ref[...] = acc_ref[...].astype(o_ref.dtype)


def gated_merge(x, g, wg, bg, br, wbr, li, *, tm, tn, name):
    m, k = x.shape
    assert m % tm == 0 and D_MODEL % tn == 0
    nj = D_MODEL // tn
    return pl.pallas_call(
        _merge_kernel,
        grid=(m // tm, nj, N_BRANCH),
        in_specs=[pl.BlockSpec((tm, k), lambda i, j, n: (i, 0)),
                  pl.BlockSpec((1, k), lambda i, j, n: (0, 0)),
                  pl.BlockSpec((1, k, tn), lambda i, j, n: (li, 0, n * nj + j)),
                  pl.BlockSpec((1, tn), lambda i, j, n: (0, n * nj + j)),
                  pl.BlockSpec((1, tm, W_A), lambda i, j, n: (n, i, 0)),
                  pl.BlockSpec((1, 1, W_A, tn), lambda i, j, n: (li, n, 0, j))],
        out_specs=pl.BlockSpec((tm, tn), lambda i, j, n: (i, j)),
        out_shape=jax.ShapeDtypeStruct((m, D_MODEL), BF16),
        scratch_shapes=[pltpu.VMEM((tm, k), BF16), pltpu.VMEM((tm, tn), F32)],
        compiler_params=_params("parallel", "arbitrary", "arbitrary"),
        name=name,
    )(x, g.reshape(1, k), wg, bg.reshape(1, -1), br, wbr)


_KV_SECTIONS = (O_BK, O_BV, O_CK, O_CV)


def _emit_heads_kernel(*refs, tt, nh):
    ns = len(_KV_SECTIONS)
    ins, outs = refs[:ns], refs[len(refs) - ns:]
    for x_ref, o_ref in zip(ins, outs):
        for h in range(nh):
            o_ref[0, pl.ds(h, tt, stride=nh), :] = x_ref[:, h * LANES:(h + 1) * LANES]


def emit_heads(u, prev, li, *, base, rows, tt, name):
    nh = H_B
    wide = nh * LANES
    assert rows % tt == 0 and base % tt == 0
    rb = base // tt
    ns = len(_KV_SECTIONS)
    in_specs = [pl.BlockSpec((tt, wide), functools.partial(lambda i, c: (rb + i, c), c=off // wide))
                for off in _KV_SECTIONS]
    args = [u] * ns
    aliases = {}
    if prev is not None:
        in_specs += [pl.BlockSpec(memory_space=pl.ANY)] * ns
        args += list(prev)
        aliases = {ns + s: s for s in range(ns)}
    return pl.pallas_call(
        functools.partial(_emit_heads_kernel, tt=tt, nh=nh),
        grid=(rows // tt,),
        in_specs=in_specs,
        out_specs=[pl.BlockSpec((1, tt * nh, LANES), lambda i: (li, i, 0))] * ns,
        out_shape=[jax.ShapeDtypeStruct((DEPTH, rows * nh, LANES), F32)] * ns,
        input_output_aliases=aliases,
        compiler_params=_params("parallel"),
        name=name,
    )(*args)


def _conv_a_kernel(ab_ref, ac_ref, ah_ref, prev_ref, w_ref, o_ref, st_ref, buf_ref, *, tt, nt):
    t = pl.program_id(1)
    km1 = CONV_A - 1

    @pl.when(t == 0)
    def _():
        buf_ref[8 - km1:8, :] = prev_ref[0]

    buf_ref[8:8 + tt, :] = ac_ref[...] * ah_ref[...]
    y = buf_ref[8 - km1:8 - km1 + tt, :] * w_ref[0:1, :]
    for k in range(1, CONV_A):
        y = y + buf_ref[8 - km1 + k:8 - km1 + k + tt, :] * w_ref[k:k + 1, :]
    o_ref[...] = (ab_ref[...] * y).astype(o_ref.dtype)
    tail = buf_ref[8 + tt - km1:8 + tt, :]
    buf_ref[8 - km1:8, :] = tail

    @pl.when(t == nt - 1)
    def _():
        st_ref[0] = tail


def conv_a_mixer(u, prev, w, *, base, nb, t_len, tt, name):
    assert t_len % tt == 0 and base % tt == 0 and tt >= CONV_A - 1
    nt = t_len // tt
    rb = base // tt

    def rows(col):
        return pl.BlockSpec((tt, W_A), lambda b, t: (rb + b * nt + t, col))

    return pl.pallas_call(
        functools.partial(_conv_a_kernel, tt=tt, nt=nt),
        grid=(nb, nt),
        in_specs=[rows(O_AB // W_A), rows(O_AC // W_A), rows(O_AH // W_A),
                  pl.BlockSpec((1, CONV_A - 1, W_A), lambda b, t: (b, 0, 0)),
                  pl.BlockSpec((CONV_A, W_A), lambda b, t: (0, 0))],
        out_specs=[pl.BlockSpec((tt, W_A), lambda b, t: (b * nt + t, 0)),
                   pl.BlockSpec((1, CONV_A - 1, W_A), lambda b, t: (b, 0, 0))],
        out_shape=[jax.ShapeDtypeStruct((nb * t_len, W_A), BF16),
                   jax.ShapeDtypeStruct((nb, CONV_A - 1, W_A), F32)],
        scratch_shapes=[pltpu.VMEM((8 + tt, W_A), F32)],
        compiler_params=_params("parallel", "arbitrary"),
        name=name,
    )(u, u, u, prev, w)


def _cum_kernel(x_ref, bias_ref, init_ref, lf_ref, cum_ref, *, logsig, nchunk, valid):
    x = x_ref[0]
    r = x.shape[0]
    if logsig:
        x = _log_sigmoid(x + bias_ref[...])
    if valid < LANES:
        lane = lax.broadcasted_iota(jnp.int32, x.shape, 1)
        x = jnp.where(lane < valid, x, 0.0)
    ri = lax.broadcasted_iota(jnp.int32, (LANES, LANES), 0)
    ci = lax.broadcasted_iota(jnp.int32, (LANES, LANES), 1)
    upper = jnp.where(ri <= ci, 1.0, 0.0).astype(F32)
    within = _dot_exact(x, upper)
    cum = within + init_ref[0]
    if nchunk > 1:
        rr = lax.broadcasted_iota(jnp.int32, (r, r), 0)
        rc = lax.broadcasted_iota(jnp.int32, (r, r), 1)
        earlier = jnp.where((rc < rr) & (rc // nchunk == rr // nchunk), 1.0, 0.0).astype(F32)
        pref = _dot_exact(earlier, within)
        cum = cum + pref[:, LANES - 1:LANES]
    lf_ref[0] = x
    cum_ref[0] = cum


def cum_rows(x, bias, init, *, logsig, nchunk, valid, name):
    nb, r, _ = x.shape
    blk = pl.BlockSpec((1, r, LANES), lambda b: (b, 0, 0))
    return pl.pallas_call(
        functools.partial(_cum_kernel, logsig=logsig, nchunk=nchunk, valid=valid),
        grid=(nb,),
        in_specs=[blk, pl.BlockSpec((r, LANES), lambda b: (0, 0)), blk],
        out_specs=[blk, blk],
        out_shape=[jax.ShapeDtypeStruct(x.shape, F32)] * 2,
        compiler_params=_params("parallel"),
        name=name,
    )(x, bias, init)


def _diff_lambda(lam_ref, lam_init):
    lp = lam_ref[...]
    s1 = jnp.sum(lp[0:1, :] * lp[1:2, :], axis=-1, keepdims=True)
    s2 = jnp.sum(lp[2:3, :] * lp[3:4, :], axis=-1, keepdims=True)
    return jnp.exp(s1) - jnp.exp(s2) + lam_init


def _split_q(q):
    lane = lax.broadcasted_iota(jnp.int32, q.shape, 1)
    q1 = jnp.where(lane < DQ_B, q, 0.0)
    q2 = jnp.where(lane >= DQ_B, q, 0.0)
    return jnp.concatenate([q1, q2], axis=0).astype(BF16)


def _diff_finish(o, lam, g, lam_init, t):
    d = o[:t] - lam * o[t:]
    ms = jnp.mean(d * d, axis=-1, keepdims=True)
    return d * lax.rsqrt(ms + EPS) * g * (1.0 - lam_init)


def _flash_step(qb, kt, vt, bias, mask, scale, m_ref, l_ref, acc_ref):
    s = _dot_nt(qb, kt) * scale
    if bias is not None:
        s = s + bias
    if mask is not None:
        s = jnp.where(mask, s, -jnp.inf)
    reps = s.shape[1] // LANES
    m_old = m_ref[...]
    m_new = jnp.maximum(m_old, jnp.max(s, axis=-1, keepdims=True))
    alpha = jnp.exp(m_old - m_new)
    p = jnp.exp(s - _lane_tile(m_new, reps))
    l_ref[...] = alpha * l_ref[...] + jnp.sum(p, axis=-1, keepdims=True)
    acc_ref[...] = alpha * acc_ref[...] + _dot(p.astype(BF16), vt)
    m_ref[...] = m_new


def _flash_init(m_ref, l_ref, acc_ref):
    m_ref[...] = jnp.full_like(m_ref, -jnp.inf)
    l_ref[...] = jnp.zeros_like(l_ref)
    acc_ref[...] = jnp.zeros_like(acc_ref)


def _rel_pos(qi, tq, tk, nrow):
    kd = (qi * tq) // tk
    row = lax.broadcasted_iota(jnp.int32, (nrow, tk), 0)
    row = jnp.where(row >= tq, row - tq, row) + (qi * tq - kd * tk)
    col = lax.broadcasted_iota(jnp.int32, (nrow, tk), 1)
    return kd, row, col


def _diff_prompt_kernel(q_ref, k_ref, v_ref, lam_ref, g_ref, o_ref,
                        kb_ref, vb_ref, m_ref, l_ref, acc_ref, *, tq, tk, lam_init):
    qi = pl.program_id(2)
    scale = DQ_B ** -0.5

    @pl.when(qi == 0)
    def _():
        kb_ref[...] = k_ref[...].astype(BF16)
        vb_ref[...] = v_ref[...].astype(BF16)

    qb = _split_q(q_ref[...])
    _flash_init(m_ref, l_ref, acc_ref)

    def tiles(kj):
        off = pl.multiple_of(kj * tk, tk)
        return kb_ref[pl.ds(off, tk), :], vb_ref[pl.ds(off, tk), :]

    def body(kj, carry):
        kt, vt = tiles(kj)
        _flash_step(qb, kt, vt, None, None, scale, m_ref, l_ref, acc_ref)
        return carry

    kd, row, col = _rel_pos(qi, tq, tk, 2 * tq)
    lax.fori_loop(0, kd, body, 0)
    kt, vt = tiles(kd)
    mask = (col // CHUNK) <= (row // CHUNK)
    _flash_step(qb, kt, vt, None, mask, scale, m_ref, l_ref, acc_ref)

    o = acc_ref[...] / l_ref[...]
    lam = _diff_lambda(lam_ref, lam_init)
    o_ref[...] = _diff_finish(o, lam, g_ref[...], lam_init, tq).astype(o_ref.dtype)


def diff_attention_prompt(u, lam_p, g, *, nb, t_len, tq, tk, lam_init, name):
    assert t_len % tk == 0 and tk % tq == 0 and tq % CHUNK == 0
    nq = t_len // tq
    hw = 2 * DQ_B
    return pl.pallas_call(
        functools.partial(_diff_prompt_kernel, tq=tq, tk=tk, lam_init=lam_init),
        grid=(nb, H_B, nq),
        in_specs=[pl.BlockSpec((tq, hw), lambda b, h, i: (b * nq + i, O_BQ // hw + h)),
                  pl.BlockSpec((t_len, hw), lambda b, h, i: (b, O_BK // hw + h)),
                  pl.BlockSpec((t_len, DV_B), lambda b, h, i: (b, O_BV // DV_B + h)),
                  pl.BlockSpec((4, DQ_B), lambda b, h, i: (0, 0)),
                  pl.BlockSpec((1, DV_B), lambda b, h, i: (0, 0))],
        out_specs=pl.BlockSpec((tq, DV_B), lambda b, h, i: (b * nq + i, h)),
        out_shape=jax.ShapeDtypeStruct((nb * t_len, H_B * DV_B), BF16),
        scratch_shapes=[pltpu.VMEM((t_len, hw), BF16), pltpu.VMEM((t_len, DV_B), BF16),
                        pltpu.VMEM((2 * tq, LANES), F32), pltpu.VMEM((2 * tq, LANES), F32),
                        pltpu.VMEM((2 * tq, DV_B), F32)],
        compiler_params=_params("parallel", "parallel", "arbitrary"),
        name=name,
    )(u, u, u, lam_p, g.reshape(1, DV_B))


def _fox_prompt_kernel(q_ref, k_ref, v_ref, qc_ref, kc_ref, o_ref,
                       kb_ref, vb_ref, m_ref, l_ref, acc_ref, *, tq, tk):
    h = pl.program_id(1)
    qi = pl.program_id(2)
    scale = DH_C ** -0.5

    @pl.when(qi == 0)
    def _():
        kb_ref[...] = k_ref[...].astype(BF16)
        vb_ref[...] = v_ref[...].astype(BF16)

    qb = q_ref[...].astype(BF16)
    qc_all = qc_ref[...]
    lane = lax.broadcasted_iota(jnp.int32, qc_all.shape, 1)
    qc = jnp.sum(jnp.where(lane == h, qc_all, 0.0), axis=-1, keepdims=True)
    qc = _lane_tile(jnp.broadcast_to(qc, (tq, LANES)), tk // LANES)
    _flash_init(m_ref, l_ref, acc_ref)

    def tiles(kj):
        off = pl.multiple_of(kj * tk, tk)
        return kb_ref[pl.ds(off, tk), :], vb_ref[pl.ds(off, tk), :], kc_ref[0, pl.ds(kj, 1), :]

    def body(kj, carry):
        kt, vt, kc = tiles(kj)
        _flash_step(qb, kt, vt, qc - kc, None, scale, m_ref, l_ref, acc_ref)
        return carry

    kd, row, col = _rel_pos(qi, tq, tk, tq)
    lax.fori_loop(0, kd, body, 0)
    kt, vt, kc = tiles(kd)
    _flash_step(qb, kt, vt, qc - kc, col <= row, scale, m_ref, l_ref, acc_ref)
    o_ref[...] = (acc_ref[...] / l_ref[...]).astype(o_ref.dtype)


def fox_attention_prompt(u, qcum, kcum, *, nb, t_len, tq, tk, name):
    assert t_len % tk == 0 and tk % tq == 0
    nq = t_len // tq
    return pl.pallas_call(
        functools.partial(_fox_prompt_kernel, tq=tq, tk=tk),
        grid=(nb, H_C, nq),
        in_specs=[pl.BlockSpec((tq, DH_C), lambda b, h, i: (b * nq + i, O_CQ // DH_C + h)),
                  pl.BlockSpec((t_len, DH_C), lambda b, h, i: (b, O_CK // DH_C + h)),
                  pl.BlockSpec((t_len, DH_C), lambda b, h, i: (b, O_CV // DH_C + h)),
                  pl.BlockSpec((tq, H_C), lambda b, h, i: (b * nq + i, 0)),
                  pl.BlockSpec((1, t_len // tk, tk), lambda b, h, i: (b * H_C + h, 0, 0))],
        out_specs=pl.BlockSpec((tq, DH_C), lambda b, h, i: (b * nq + i, h)),
        out_shape=jax.ShapeDtypeStruct((nb * t_len, H_C * DH_C), BF16),
        scratch_shapes=[pltpu.VMEM((t_len, DH_C), BF16), pltpu.VMEM((t_len, DH_C), BF16),
                        pltpu.VMEM((tq, LANES), F32), pltpu.VMEM((tq, LANES), F32),
                        pltpu.VMEM((tq, DH_C), F32)],
        compiler_params=_params("parallel", "parallel", "arbitrary"),
        name=name,
    )(u, u, u, qcum, kcum)


def _two_part_softmax_pv(s_c, s_n, vc, vn):
    m = jnp.maximum(jnp.max(s_c, axis=-1, keepdims=True), jnp.max(s_n, axis=-1, keepdims=True))
    p_c = jnp.exp(s_c - m)
    p_n = jnp.exp(s_n - m)
    l = jnp.sum(p_c, axis=-1, keepdims=True) + jnp.sum(p_n, axis=-1, keepdims=True)
    o = _dot(p_c.astype(BF16), vc) + _dot(p_n.astype(BF16), vn)
    return o / l


def _diff_sample_kernel(q_ref, kn_ref, vn_ref, kc_ref, vc_ref, lam_ref, g_ref, o_ref,
                        *, ts, past, lam_init):
    scale = DQ_B ** -0.5
    lam = _diff_lambda(lam_ref, lam_init)
    row = lax.broadcasted_iota(jnp.int32, (2 * ts, ts), 0)
    col = lax.broadcasted_iota(jnp.int32, (2 * ts, ts), 1)
    row = jnp.where(row >= ts, row - ts, row)
    mask = ((past + col) // CHUNK) <= ((past + row) // CHUNK)
    for h in range(H_B):
        sl = slice(h * DV_B, (h + 1) * DV_B)
        qb = _split_q(q_ref[:, sl])
        kc = kc_ref[0, 0, pl.ds(h, past, stride=H_B), :].astype(BF16)
        vc = vc_ref[0, 0, pl.ds(h, past, stride=H_B), :].astype(BF16)
        kn = kn_ref[:, sl].astype(BF16)
        vn = vn_ref[:, sl].astype(BF16)
        s_c = _dot_nt(qb, kc) * scale
        s_n = jnp.where(mask, _dot_nt(qb, kn) * scale, -jnp.inf)
        o = _two_part_softmax_pv(s_c, s_n, vc, vn)
        o_ref[:, sl] = _diff_finish(o, lam, g_ref[...], lam_init, ts).astype(o_ref.dtype)


def diff_attention_sample(u, kc, vc, li, lam_p, g, *, base, nb, ts, past, lam_init, name):
    assert (past - 1) // CHUNK <= past // CHUNK and base % ts == 0
    rb = base // ts
    wide = H_B * DV_B
    new = lambda col: pl.BlockSpec((ts, wide), lambda b: (rb + b, col))
    cache = pl.BlockSpec((1, 1, past * H_B, LANES), lambda b: (li, b, 0, 0))
    return pl.pallas_call(
        functools.partial(_diff_sample_kernel, ts=ts, past=past, lam_init=lam_init),
        grid=(nb,),
        in_specs=[new(O_BQ // wide), new(O_BK // wide), new(O_BV // wide), cache, cache,
                  pl.BlockSpec((4, DQ_B), lambda b: (0, 0)),
                  pl.BlockSpec((1, DV_B), lambda b: (0, 0))],
        out_specs=pl.BlockSpec((ts, wide), lambda b: (b, 0)),
        out_shape=jax.ShapeDtypeStruct((nb * ts, wide), BF16),
        compiler_params=_params("parallel"),
        name=name,
    )(u, u, u, kc, vc, lam_p, g.reshape(1, DV_B))


def _fox_sample_kernel(q_ref, kn_ref, vn_ref, kc_ref, vc_ref, qcum_ref, kcum_c_ref, kcum_n_ref,
                       o_ref, *, ts, past):
    scale = DH_C ** -0.5
    row = lax.broadcasted_iota(jnp.int32, (ts, ts), 0)
    col = lax.broadcasted_iota(jnp.int32, (ts, ts), 1)
    mask = col <= row
    for h in range(H_C):
        sl = slice(h * DH_C, (h + 1) * DH_C)
        qb = q_ref[:, sl].astype(BF16)
        kc = kc_ref[0, 0, pl.ds(h, past, stride=H_C), :].astype(BF16)
        vc = vc_ref[0, 0, pl.ds(h, past, stride=H_C), :].astype(BF16)
        kn = kn_ref[:, sl].astype(BF16)
        vn = vn_ref[:, sl].astype(BF16)
        qc = qcum_ref[0, :, h:h + 1]
        kcc = kcum_c_ref[0, h:h + 1, :]
        kcn = kcum_n_ref[0, h:h + 1, 0:ts]
        s_c = _dot_nt(qb, kc) * scale + (qc - kcc)
        s_n = jnp.where(mask, _dot_nt(qb, kn) * scale + (qc - kcn), -jnp.inf)
        o_ref[:, sl] = _two_part_softmax_pv(s_c, s_n, vc, vn).astype(o_ref.dtype)


def fox_attention_sample(u, kc, vc, li, qcum, kcum_c, kcum_n, *, base, nb, ts, past, name):
    assert base % ts == 0
    rb = base // ts
    wide = H_C * DH_C
    new = lambda col: pl.BlockSpec((ts, wide), lambda b: (rb + b, col))
    cache = pl.BlockSpec((1, 1, past * H_C, LANES), lambda b: (li, b, 0, 0))
    return pl.pallas_call(
        functools.partial(_fox_sample_kernel, ts=ts, past=past),
        grid=(nb,),
        in_specs=[new(O_CQ // wide), new(O_CK // wide), new(O_CV // wide), cache, cache,
                  pl.BlockSpec((1, ts, H_C), lambda b: (b, 0, 0)),
                  pl.BlockSpec((1, H_C, past), lambda b: (b, 0, 0)),
                  pl.BlockSpec((1, 8, LANES), lambda b: (b, 0, 0))],
        out_specs=pl.BlockSpec((ts, wide), lambda b: (b, 0)),
        out_shape=jax.ShapeDtypeStruct((nb * ts, wide), BF16),
        compiler_params=_params("parallel"),
        name=name,
    )(u, u, u, kc, vc, qcum, kcum_c, kcum_n)


def _pair(v, j):
    lane = lax.broadcasted_iota(jnp.int32, (v.shape[0], 2 * P_D), 1)
    c0 = MISC_DT + 2 * j
    return jnp.where(lane < P_D, v[:, c0:c0 + 1], v[:, c0 + 1:c0 + 2])


def _ssd_kernel(xbc_ref, z_ref, misc_ref, cprev_ref, sprev_ref, cw_ref, cb_ref, dtb_ref, alog_ref,
                dskip_ref, ng_ref, o_ref, sout_ref, cout_ref, buf_ref, st_ref, *, tb, nc):
    c = pl.program_id(1)
    L = SSD_L
    km1 = CONV_D_W - 1

    @pl.when(c == 0)
    def _():
        buf_ref[8 - km1:8, :] = cprev_ref[0]
        st_ref[...] = sprev_ref[0]

    if tb < L:
        buf_ref[8 + tb:8 + L, :] = jnp.zeros((L - tb, CONV_D), F32)
    buf_ref[8:8 + tb, :] = xbc_ref[...]
    conv = buf_ref[8 - km1:8 - km1 + L, :] * cw_ref[0:1, :]
    for k in range(1, CONV_D_W):
        conv = conv + buf_ref[8 - km1 + k:8 - km1 + k + L, :] * cw_ref[k:k + 1, :]
    tail = buf_ref[8 + tb - km1:8 + tb, :]
    buf_ref[8 - km1:8, :] = tail
    conv = conv + cb_ref[...]
    act = conv * _sigmoid(conv)
    xs = act[:, :D_INNER]
    bm = act[:, D_INNER:D_INNER + G_D * N_D].astype(BF16)
    cm = act[:, D_INNER + G_D * N_D:].astype(BF16)

    if tb < L:
        misc = jnp.concatenate([misc_ref[:, 0:LANES], jnp.zeros((L - tb, LANES), F32)], axis=0)
        zz = jnp.concatenate([z_ref[...], jnp.zeros((L - tb, D_INNER), F32)], axis=0)
    else:
        misc = misc_ref[:, 0:LANES]
        zz = z_ref[...]
    dt = _softplus(misc + dtb_ref[...])
    if tb < L:
        rowi = lax.broadcasted_iota(jnp.int32, dt.shape, 0)
        dt = jnp.where(rowi < tb, dt, 0.0)
    da = dt * (-jnp.exp(alog_ref[...]))
    ri = lax.broadcasted_iota(jnp.int32, (L, L), 0)
    ci = lax.broadcasted_iota(jnp.int32, (L, L), 1)
    causal = ci <= ri
    a_cum = _dot_exact(jnp.where(causal, 1.0, 0.0).astype(F32), da)
    a_cum_t = a_cum.T
    a_last = a_cum[L - 1:L, :]
    e_cum = jnp.exp(a_cum)
    e_end = jnp.exp(a_last - a_cum)
    e_last = jnp.exp(a_last)
    lane2 = lax.broadcasted_iota(jnp.int32, (L, 2 * P_D), 1)
    rows2 = lax.broadcasted_iota(jnp.int32, (2 * P_D, N_D), 0)

    ys = []
    for j in range(H_D // 2):
        g = (2 * j) // (H_D // G_D)
        bg = bm[:, g * N_D:(g + 1) * N_D]
        cg = cm[:, g * N_D:(g + 1) * N_D]
        cb = _dot_nt(cg, bg)
        x_pair = xs[:, j * 2 * P_D:(j + 1) * 2 * P_D]
        xdt = x_pair * _pair(dt, j)
        y = jnp.zeros((L, 2 * P_D), F32)
        for e in range(2):
            hl = MISC_DT + 2 * j + e
            seg = a_cum[:, hl:hl + 1] - a_cum_t[hl:hl + 1, :]
            decay = jnp.exp(jnp.where(causal, seg, -jnp.inf))
            half = (lane2 < P_D) if e == 0 else (lane2 >= P_D)
            xm = jnp.where(half, xdt, 0.0).astype(BF16)
            y = y + _dot((cb * decay).astype(BF16), xm)
        s_pair = st_ref[j * 2 * P_D:(j + 1) * 2 * P_D, :]
        y = y + _dot_nt(cg, s_pair.astype(BF16)) * _pair(e_cum, j)
        y = y + dskip_ref[:, j * 2 * P_D:(j + 1) * 2 * P_D] * x_pair
        ys.append(y)
        xw = xdt * _pair(e_end, j)
        c0 = MISC_DT + 2 * j
        keep = jnp.where(rows2 < P_D, e_last[:, c0:c0 + 1], e_last[:, c0 + 1:c0 + 2])
        st_ref[j * 2 * P_D:(j + 1) * 2 * P_D, :] = s_pair * keep + _dot(xw.T.astype(BF16), bg)

    y = jnp.concatenate(ys, axis=1)
    gated = y * (zz * _sigmoid(zz))
    gw = D_INNER // G_D
    outs = []
    for g in range(G_D):
        gg = gated[:, g * gw:(g + 1) * gw]
        ms = jnp.mean(gg * gg, axis=-1, keepdims=True)
        outs.append(gg * lax.rsqrt(ms + EPS))
    out = jnp.concatenate(outs, axis=1) * ng_ref[...]
    o_ref[...] = out[0:tb, :].astype(o_ref.dtype)

    @pl.when(c == nc - 1)
    def _():
        sout_ref[0] = st_ref[...]
        cout_ref[0] = tail


def mamba2_mixer(u, conv_prev, ssm_prev, cw, cb, dtb, alog, dskip, ng, *, base, nb, t_len, tb, name):
    assert t_len % tb == 0 and base % tb == 0 and tb <= SSD_L and tb >= CONV_D_W - 1
    assert t_len == tb or tb == SSD_L
    nc = t_len // tb
    rb = base // tb
    rows = lambda w, col: pl.BlockSpec((tb, w), lambda b, c: (rb + b * nc + c, col))
    full = lambda shape: pl.BlockSpec(shape, lambda b, c: tuple(0 for _ in shape))
    return pl.pallas_call(
        functools.partial(_ssd_kernel, tb=tb, nc=nc),
        grid=(nb, nc),
        in_specs=[rows(CONV_D, O_XBC // CONV_D), rows(D_INNER, O_DZ // D_INNER),
                  rows(MISC_W, O_MISC // MISC_W),
                  pl.BlockSpec((1, CONV_D_W - 1, CONV_D), lambda b, c: (b, 0, 0)),
                  pl.BlockSpec((1, H_D * P_D, N_D), lambda b, c: (b, 0, 0)),
                  full((CONV_D_W, CONV_D)), full((1, CONV_D)), full((1, LANES)), full((1, LANES)),
                  full((1, D_INNER)), full((1, D_INNER))],
        out_specs=[pl.BlockSpec((tb, D_INNER), lambda b, c: (b * nc + c, 0)),
                   pl.BlockSpec((1, H_D * P_D, N_D), lambda b, c: (b, 0, 0)),
                   pl.BlockSpec((1, CONV_D_W - 1, CONV_D), lambda b, c: (b, 0, 0))],
        out_shape=[jax.ShapeDtypeStruct((nb * t_len, D_INNER), BF16),
                   jax.ShapeDtypeStruct((nb, H_D * P_D, N_D), F32),
                   jax.ShapeDtypeStruct((nb, CONV_D_W - 1, CONV_D), F32)],
        scratch_shapes=[pltpu.VMEM((8 + SSD_L, CONV_D), F32), pltpu.VMEM((H_D * P_D, N_D), F32)],
        compiler_params=_params("parallel", "arbitrary"),
        name=name,
    )(u, u, u, conv_prev, ssm_prev, cw, cb, dtb, alog, dskip, ng)


def _head_lanes(v):
    return jnp.zeros((1, LANES), F32).at[0, MISC_DT:MISC_DT + H_D].set(v.astype(F32))


def kernel(x_prompt, x_sample, state_conv_a, cache_diff_k, cache_diff_v, cache_fox_k, cache_fox_v,
           cache_fox_logf, state_ssm_conv, state_ssm, norm_mix_g, w_in, b_fox_f, b_gate, conv_a_w,
           diff_lambda, diff_subln_g, ssm_conv_w, ssm_conv_b, ssm_dt_bias, ssm_a_log, ssm_d,
           ssm_norm_g, w_br, w_out, norm_ffn_g, w_ff1, w_ff2, norm_final_g):
    pb, pt, _ = x_prompt.shape
    sb, st, _ = x_sample.shape
    past = cache_diff_k.shape[2]
    mp, ms = pb * pt, sb * st
    tk_attn = 512
    x = jnp.concatenate([x_prompt.reshape(mp, D_MODEL), x_sample.reshape(ms, D_MODEL)], axis=0)
    tm = 768
    assert (mp + ms) % tm == 0

    w_proj, w_gate = repack_w_in(w_in)
    w_br_b, w_out_b = w_br.astype(BF16), w_out.astype(BF16)
    w_ff1_b, w_ff2_b = w_ff1.astype(BF16), w_ff2.astype(BF16)
    ck_b = cache_diff_k.reshape(DEPTH, sb, past * H_B, LANES)
    cv_b = cache_diff_v.reshape(DEPTH, sb, past * H_B, LANES)
    ck_c = cache_fox_k.reshape(DEPTH, sb, past * H_C, LANES)
    cv_c = cache_fox_v.reshape(DEPTH, sb, past * H_C, LANES)

    kv_p = kv_s = None
    outs_p = [[] for _ in range(8)]
    outs_s = [[] for _ in range(8)]
    for li in range(DEPTH):
        lam_init = 0.8 - 0.6 * math.exp(-0.3 * li)
        u = norm_matmul(x, norm_mix_g[li], w_proj, li, tm=tm, tn=1280, relu2=False,
                        out_dtype=F32, name=f"in_proj_{li}")
        kv_p = emit_heads(u, kv_p, li, base=0, rows=mp, tt=512, name=f"kv_p_{li}")
        kv_s = emit_heads(u, kv_s, li, base=mp, rows=ms, tt=ms, name=f"kv_s_{li}")

        a_p, ca_p = conv_a_mixer(u, jnp.zeros((pb, CONV_A - 1, W_A), F32), conv_a_w[li],
                                 base=0, nb=pb, t_len=pt, tt=512, name=f"conv_a_p_{li}")
        a_s, ca_s = conv_a_mixer(u, state_conv_a[li], conv_a_w[li],
                                 base=mp, nb=sb, t_len=st, tt=st, name=f"conv_a_s_{li}")

        b_p = diff_attention_prompt(u, diff_lambda[li], diff_subln_g[li], nb=pb, t_len=pt, tq=256,
                                    tk=tk_attn, lam_init=lam_init, name=f"diff_p_{li}")
        b_s = diff_attention_sample(u, ck_b, cv_b, li, diff_lambda[li], diff_subln_g[li], base=mp,
                                    nb=sb, ts=st, past=past, lam_init=lam_init, name=f"diff_s_{li}")

        cf = u[:, O_MISC + MISC_CF:O_MISC + MISC_CF + H_C]
        ncp = pt // LANES
        bias_p = jnp.broadcast_to(jnp.repeat(b_fox_f[li], ncp)[:, None], (H_C * ncp, LANES))
        cf_p = jnp.transpose(cf[:mp].reshape(pb, pt, H_C), (0, 2, 1))
        lf_p, cum_p = cum_rows(cf_p.reshape(pb, H_C * ncp, LANES), bias_p,
                               jnp.zeros((pb, H_C * ncp, LANES), F32),
                               logsig=True, nchunk=ncp, valid=LANES, name=f"cum_p_{li}")
        lf_p = lf_p.reshape(pb, H_C, pt)
        cum_p = cum_p.reshape(pb, H_C, pt)
        c_p = fox_attention_prompt(u, jnp.transpose(cum_p, (0, 2, 1)).reshape(mp, H_C),
                                   cum_p.reshape(pb * H_C, pt // tk_attn, tk_attn), nb=pb, t_len=pt,
                                   tq=512, tk=tk_attn, name=f"fox_p_{li}")
        ncc = past // LANES
        rc = H_C * ncc
        lfc = jnp.transpose(cache_fox_logf[li], (0, 2, 1)).reshape(sb, rc, LANES)
        lfc = jnp.pad(lfc, ((0, 0), (0, LANES - rc), (0, 0)))
        _, cum_c = cum_rows(lfc, jnp.zeros((LANES, LANES), F32), jnp.zeros((sb, LANES, LANES), F32),
                            logsig=False, nchunk=ncc, valid=LANES, name=f"cum_c_{li}")
        cum_c = cum_c[:, :rc].reshape(sb, H_C, past)
        cf_s = jnp.transpose(cf[mp:].reshape(sb, st, H_C), (0, 2, 1))
        cf_s = jnp.pad(cf_s, ((0, 0), (0, 8 - H_C), (0, LANES - st)))
        bias_s = jnp.broadcast_to(jnp.pad(b_fox_f[li], (0, 8 - H_C))[:, None], (8, LANES))
        init_s = jnp.broadcast_to(jnp.pad(cum_c[:, :, -1], ((0, 0), (0, 8 - H_C)))[:, :, None],
                                  (sb, 8, LANES))
        lf_s, cum_s = cum_rows(cf_s, bias_s, init_s, logsig=True, nchunk=1, valid=st,
                               name=f"cum_s_{li}")
        c_s = fox_attention_sample(u, ck_c, cv_c, li, jnp.transpose(cum_s[:, :H_C, :st], (0, 2, 1)),
                                   cum_c, cum_s, base=mp, nb=sb, ts=st, past=past, name=f"fox_s_{li}")

        dtb = _head_lanes(ssm_dt_bias[li])
        alog = _head_lanes(ssm_a_log[li])
        dskip = jnp.repeat(ssm_d[li].astype(F32), P_D)[None, :]
        ssm_args = (ssm_conv_w[li], ssm_conv_b[li][None, :], dtb, alog, dskip, ssm_norm_g[li][None, :])
        d_p, ssm_p, cs_p = mamba2_mixer(u, jnp.zeros((pb, CONV_D_W - 1, CONV_D), F32),
                                        jnp.zeros((pb, H_D * P_D, N_D), F32), *ssm_args,
                                        base=0, nb=pb, t_len=pt, tb=SSD_L, name=f"ssd_p_{li}")
        d_s, ssm_s, cs_s = mamba2_mixer(u, state_ssm_conv[li],
                                        state_ssm[li].reshape(sb, H_D * P_D, N_D), *ssm_args,
                                        base=mp, nb=sb, t_len=st, tb=st, name=f"ssd_s_{li}")

        br = jnp.stack([jnp.concatenate([p, s], axis=0)
                        for p, s in ((a_p, a_s), (b_p, b_s), (c_p, c_s), (d_p, d_s))])
        merged = gated_merge(x, norm_mix_g[li], w_gate, b_gate[li], br, w_br_b, li,
                             tm=tm, tn=1024, name=f"merge_{li}")
        x_mid = matmul_res(merged, w_out_b, li, x, norm_final_g, tm=tm, tk=1024,
                           final_norm=False, name=f"out_proj_{li}")
        hid = norm_matmul(x_mid, norm_ffn_g[li], w_ff1_b, li, tm=tm, tn=1024, relu2=True,
                          out_dtype=BF16, name=f"ff1_{li}")

        outs_p[0].append(ca_p)
        outs_s[0].append(ca_s)
        outs_p[5].append(jnp.transpose(lf_p, (0, 2, 1)))
        outs_s[5].append(jnp.transpose(lf_s[:, :H_C, :st], (0, 2, 1)))
        outs_p[6].append(cs_p)
        outs_s[6].append(cs_s)
        outs_p[7].append(ssm_p.reshape(pb, H_D, P_D, N_D))
        outs_s[7].append(ssm_s.reshape(sb, H_D, P_D, N_D))

        x = matmul_res(hid, w_ff2_b, li, x_mid, norm_final_g, tm=tm, tk=1024,
                       final_norm=(li == DEPTH - 1), name=f"ff2_{li}")

    y_prompt = x[:mp].reshape(pb, pt, D_MODEL)
    y_sample = x[mp:].reshape(sb, st, D_MODEL)
    res = [y_prompt, y_sample]
    for idx in range(8):
        if 1 <= idx <= 4:
            res.append(kv_p[idx - 1].reshape(DEPTH, pb, pt, H_B, LANES))
            res.append(kv_s[idx - 1].reshape(DEPTH, sb, st, H_B, LANES))
        else:
            res.append(jnp.stack(outs_p[idx]))
            res.append(jnp.stack(outs_s[idx]))
    return tuple(res)
```
